```python
import jax, jax.numpy as jnp
from jax import lax
import numpy as np

D_MODEL = 2048
BATCH = 2
SEQ = 8192
DEPTH = 2

N_MIXERS = 2
N_ATTN_LAYERS = (DEPTH + 1) // 2
N_DN_LAYERS = DEPTH // 2
RMS_EPS = 1e-6

N_HEADS = 16
N_KV_HEADS = 4
HEAD_DIM = 128
ROT_FRACTION = 4
ROPE_THETA = 500000.0
IDX_HEADS = 16
IDX_DIM = 64
INDEX_TOPK = 256
Q_BLOCK = 128
ATTN_SPLITS = (N_HEADS * HEAD_DIM, N_KV_HEADS * HEAD_DIM, N_KV_HEADS * HEAD_DIM, IDX_HEADS * IDX_DIM, IDX_DIM, IDX_HEADS)
ATTN_PROJ = sum(ATTN_SPLITS)

DN_QK_HEADS = 16
DN_V_HEADS = 32
DN_HEAD_DIM = 128
DN_KEY_DIM = DN_QK_HEADS * DN_HEAD_DIM
DN_VAL_DIM = DN_V_HEADS * DN_HEAD_DIM
DN_CONV_WIDTH = 4
DN_CHUNK = 64
DN_CONV_CH = 2 * DN_KEY_DIM + DN_VAL_DIM
DN_SPLITS = (DN_KEY_DIM, DN_KEY_DIM, DN_VAL_DIM, DN_VAL_DIM, DN_V_HEADS, DN_V_HEADS)
DN_PROJ = sum(DN_SPLITS)

PEER_HEADS = 8
PEER_NKEYS = 128
PEER_EXPERTS = PEER_NKEYS * PEER_NKEYS
PEER_QDIM = 256
PEER_TOPK = 16
PEER_BLOCK = 128

PLE_DIM = 256

kernel_name = 'hybrid_dsa_gdn_peer_ple'

F32 = jnp.float32


def split_cols(y, sizes):
    offs = [int(o) for o in np.cumsum(sizes)[:-1]]
    return jnp.split(y, offs, axis=-1)


def rmsnorm(x, gain):
    xf = x.astype(F32)
    y = xf * lax.rsqrt(jnp.mean(xf * xf, axis=-1, keepdims=True) + RMS_EPS)
    return (y * gain.astype(F32)).astype(x.dtype)


def l2norm(x):
    xf = x.astype(F32)
    return xf * lax.rsqrt(jnp.sum(xf * xf, axis=-1, keepdims=True) + RMS_EPS)


def rope_partial(x, pos):
    d = x.shape[-1]
    rot = d // ROT_FRACTION
    half = rot // 2
    inv_freq = ROPE_THETA ** (-jnp.arange(half, dtype=F32) * (2.0 / rot))
    ang = pos.astype(F32)[:, None] * inv_freq[None, :]
    cos = jnp.cos(ang)[None, :, None, :]
    sin = jnp.sin(ang)[None, :, None, :]
    xf = x.astype(F32)
    x1, x2, rest = xf[..., :half], xf[..., half:rot], xf[..., rot:]
    return jnp.concatenate([x1 * cos - x2 * sin, x2 * cos + x1 * sin, rest], axis=-1).astype(x.dtype)


def dsa_attention(xn, w_in, q_gain, k_gain, w_out):
    B, T, _ = xn.shape
    pos = jnp.arange(T, dtype=jnp.int32)
    q, k, v, iq, ik, iw = split_cols(xn @ w_in, ATTN_SPLITS)
    q = rope_partial(rmsnorm(q.reshape(B, T, N_HEADS, HEAD_DIM), q_gain), pos)
    k = rope_partial(rmsnorm(k.reshape(B, T, N_KV_HEADS, HEAD_DIM), k_gain), pos)
    v = v.reshape(B, T, N_KV_HEADS, HEAD_DIM)
    iq = rope_partial(iq.reshape(B, T, IDX_HEADS, IDX_DIM), pos).astype(F32)
    ik = rope_partial(ik.reshape(B, T, 1, IDX_DIM), pos)[:, :, 0].astype(F32)
    iw = iw.astype(F32) * (IDX_HEADS ** -0.5)
    topk = min(INDEX_TOPK, T // 4)
    nb = T // Q_BLOCK
    group = N_HEADS // N_KV_HEADS

    def to_blocks(a):
        return jnp.moveaxis(a.reshape(B, nb, Q_BLOCK, *a.shape[2:]), 1, 0)

    def block(args):
        qb, iqb, iwb, pb = args
        s = jax.nn.relu(jnp.einsum('bqhd,bsd->bqhs', iqb, ik) * (IDX_DIM ** -0.5))
        score = jnp.einsum('bqhs,bqh->bqs', s, iwb)
        causal = pos[None, None, :] <= pb[None, :, None]
        score = jnp.where(causal, score, -jnp.inf)
        _, idx = lax.top_k(score, topk)
        valid = idx <= pb[None, :, None]
        kg = jax.vmap(lambda kb, ib: kb[ib])(k, idx)
        vg = jax.vmap(lambda vb, ib: vb[ib])(v, idx)
        qg = qb.reshape(B, Q_BLOCK, N_KV_HEADS, group, HEAD_DIM).astype(F32)
        logits = jnp.einsum('bqngd,bqknd->bqngk', qg, kg.astype(F32)) * (HEAD_DIM ** -0.5)
        logits = jnp.where(valid[:, :, None, None, :], logits, -jnp.inf)
        probs = jax.nn.softmax(logits, axis=-1)
        ob = jnp.einsum('bqngk,bqknd->bqngd', probs, vg.astype(F32))
        return ob.reshape(B, Q_BLOCK, N_HEADS * HEAD_DIM).astype(xn.dtype)

    o = lax.map(block, (to_blocks(q), to_blocks(iq), to_blocks(iw), pos.reshape(nb, Q_BLOCK)))
    o = jnp.moveaxis(o, 0, 1).reshape(B, T, N_HEADS * HEAD_DIM)
    return o @ w_out


def causal_depthwise_conv(x, w):
    return lax.conv_general_dilated(
        x, w[:, None, :].astype(x.dtype), window_strides=(1,),
        padding=[(DN_CONV_WIDTH - 1, 0)], dimension_numbers=('NWC', 'WIO', 'NWC'),
        feature_group_count=x.shape[-1])


def chunk_gated_delta_rule(q, k, v, g, beta):
    B, T, H, dk = k.shape
    dv = v.shape[-1]
    C = DN_CHUNK
    N = T // C

    def chunks(a):
        return jnp.moveaxis(a.reshape(B, N, C, H, *a.shape[3:]), 3, 1)

    q = chunks(q) * (dk ** -0.5)
    k = chunks(k)
    v = chunks(v)
    g = jnp.cumsum(chunks(g), axis=-1)
    beta = chunks(beta)
    ar = jnp.arange(C)
    tril = ar[:, None] >= ar[None, :]
    strict = ar[:, None] > ar[None, :]
    decay = jnp.exp(jnp.where(tril, g[..., :, None] - g[..., None, :], -jnp.inf))
    k_beta = k * beta[..., None]
    a = jnp.where(strict, jnp.einsum('bhncd,bhnsd->bhncs', k_beta, k) * decay, 0.0)
    m = a + jnp.eye(C, dtype=a.dtype)
    rhs = jnp.concatenate([v * beta[..., None], k_beta * jnp.exp(g)[..., None]], axis=-1)
    sol = lax.linalg.triangular_solve(m, rhs, left_side=True, lower=True, unit_diagonal=True)
    u, w = sol[..., :dv], sol[..., dv:]
    qk = jnp.einsum('bhncd,bhnsd->bhncs', q, k) * decay

    def step(state, xs):
        q_i, k_i, u_i, w_i, qk_i, g_i = xs
        v_new = u_i - jnp.einsum('bhcd,bhde->bhce', w_i, state)
        o_i = (jnp.einsum('bhcd,bhde->bhce', q_i * jnp.exp(g_i)[..., None], state)
               + jnp.einsum('bhcs,bhse->bhce', qk_i, v_new))
        g_last = g_i[..., -1:]
        state = (state * jnp.exp(g_last)[..., None]
                 + jnp.einsum('bhcd,bhce->bhde', k_i * jnp.exp(g_last - g_i)[..., None], v_new))
        return state, o_i

    xs = tuple(jnp.moveaxis(t, 2, 0) for t in (q, k, u, w, qk, g))
    state0 = jnp.zeros((B, H, dk, dv), F32)
    _, o = lax.scan(step, state0, xs)
    return jnp.transpose(o, (1, 0, 3, 2, 4)).reshape(B, T, H, dv)


def gated_deltanet(xn, w_in, conv_w, a_log, dt_bias, norm_gain, w_out):
    B, T, _ = xn.shape
    q, k, v, z, b, a = split_cols(xn @ w_in, DN_SPLITS)
    qkv = jax.nn.silu(causal_depthwise_conv(jnp.concatenate([q, k, v], axis=-1), conv_w))
    q, k, v = split_cols(qkv, (DN_KEY_DIM, DN_KEY_DIM, DN_VAL_DIM))
    rep = DN_V_HEADS // DN_QK_HEADS
    q = jnp.repeat(l2norm(q.reshape(B, T, DN_QK_HEADS, DN_HEAD_DIM)), rep, axis=2)
    k = jnp.repeat(l2norm(k.reshape(B, T, DN_QK_HEADS, DN_HEAD_DIM)), rep, axis=2)
    v = v.reshape(B, T, DN_V_HEADS, DN_HEAD_DIM).astype(F32)
    beta = jax.nn.sigmoid(b.astype(F32))
    g = -jnp.exp(a_log.astype(F32)) * jax.nn.softplus(a.astype(F32) + dt_bias.astype(F32))
    o = chunk_gated_delta_rule(q, k, v, g, beta)
    o = rmsnorm(o, norm_gain) * jax.nn.silu(z.reshape(B, T, DN_V_HEADS, DN_HEAD_DIM).astype(F32))
    return o.reshape(B, T, DN_VAL_DIM).astype(xn.dtype) @ w_out


def peer_ffn(xn, w_q, sub_keys, u_tab, v_tab):
    B, T, D = xn.shape
    q = (xn @ w_q).reshape(B, T, PEER_HEADS, 2, PEER_QDIM // 2).astype(F32)
    s = jnp.einsum('bthpd,hpkd->bthpk', q, sub_keys.astype(F32))
    s_top, i_top = lax.top_k(s, PEER_TOPK)
    cand_s = (s_top[..., 0, :, None] + s_top[..., 1, None, :]).reshape(B, T, PEER_HEADS, PEER_TOPK * PEER_TOPK)
    cand_i = (i_top[..., 0, :, None] * PEER_NKEYS + i_top[..., 1, None, :]).reshape(B, T, PEER_HEADS, PEER_TOPK * PEER_TOPK)
    f_s, f_pos = lax.top_k(cand_s, PEER_TOPK)
    eid = jnp.take_along_axis(cand_i, f_pos, axis=-1)
    gate = jax.nn.softmax(f_s, axis=-1)
    nsel = PEER_HEADS * PEER_TOPK
    nb = (B * T) // PEER_BLOCK
    xb = xn.reshape(nb, PEER_BLOCK, D)
    eb = eid.reshape(nb, PEER_BLOCK, nsel)
    gb = gate.reshape(nb, PEER_BLOCK, nsel)

    def block(args):
        x_i, e_i, g_i = args
        act = jnp.einsum('nkd,nd->nk', u_tab[e_i], x_i)
        coef = (g_i * jax.nn.gelu(act.astype(F32), approximate=False)).astype(x_i.dtype)
        return jnp.einsum('nk,nkd->nd', coef, v_tab[e_i])

    y = lax.map(block, (xb, eb, gb))
    return y.reshape(B, T, D)


def setup_inputs(seed: int = 0) -> dict:
    key = jax.random.key(seed)
    ks = jax.random.split(key, 24)

    def nrm(k, shape, scale):
        return jax.random.normal(k, shape, F32) * scale

    def gain(k, shape):
        return 1.0 + 0.05 * jax.random.normal(k, shape, F32)

    NA, NB = N_ATTN_LAYERS, N_DN_LAYERS
    return {
        'x': nrm(ks[0], (BATCH, SEQ, D_MODEL), 1.0),
        'p': nrm(ks[1], (DEPTH, BATCH, SEQ, PLE_DIM), 1.0),
        'norm_mix': gain(ks[2], (DEPTH, D_MODEL)),
        'norm_ffn': gain(ks[3], (DEPTH, D_MODEL)),
        'norm_ple': gain(ks[4], (DEPTH, D_MODEL)),
        'attn_w_in': nrm(ks[5], (NA, D_MODEL, ATTN_PROJ), D_MODEL ** -0.5),
        'attn_q_norm': gain(ks[6], (NA, HEAD_DIM)),
        'attn_k_norm': gain(ks[7], (NA, HEAD_DIM)),
        'attn_w_out': nrm(ks[8], (NA, N_HEADS * HEAD_DIM, D_MODEL), (N_HEADS * HEAD_DIM) ** -0.5),
        'dn_w_in': nrm(ks[9], (NB, D_MODEL, DN_PROJ), D_MODEL ** -0.5),
        'dn_conv': nrm(ks[10], (NB, DN_CONV_WIDTH, DN_CONV_CH), DN_CONV_WIDTH ** -0.5),
        'dn_a_log': jnp.log(jax.random.uniform(ks[11], (NB, DN_V_HEADS), F32, 1.0, 16.0)),
        'dn_dt_bias': nrm(ks[12], (NB, DN_V_HEADS), 0.1),
        'dn_norm': gain(ks[13], (NB, DN_HEAD_DIM)),
        'dn_w_out': nrm(ks[14], (NB, DN_VAL_DIM, D_MODEL), DN_VAL_DIM ** -0.5),
        'peer_w_q': nrm(ks[15], (DEPTH, D_MODEL, PEER_HEADS * PEER_QDIM), D_MODEL ** -0.5),
        'peer_keys': nrm(ks[16], (DEPTH, PEER_HEADS, 2, PEER_NKEYS, PEER_QDIM // 2), (PEER_QDIM // 2) ** -0.5),
        'peer_u': nrm(ks[17], (DEPTH, PEER_EXPERTS, D_MODEL), D_MODEL ** -0.5),
        'peer_v': nrm(ks[18], (DEPTH, PEER_EXPERTS, D_MODEL), (PEER_HEADS * PEER_TOPK) ** -0.5),
        'ple_w_in': nrm(ks[19], (DEPTH, PLE_DIM, D_MODEL), PLE_DIM ** -0.5),
        'ple_w_gate': nrm(ks[20], (DEPTH, D_MODEL, D_MODEL), D_MODEL ** -0.5),
    }


def reference(x, p, norm_mix, norm_ffn, norm_ple, attn_w_in, attn_q_norm, attn_k_norm, attn_w_out,
              dn_w_in, dn_conv, dn_a_log, dn_dt_bias, dn_norm, dn_w_out,
              peer_w_q, peer_keys, peer_u, peer_v, ple_w_in, ple_w_gate):
    h = x
    for i in range(DEPTH):
        hn = rmsnorm(h, norm_mix[i])
        j = i // N_MIXERS
        if i % N_MIXERS == 0:
            mix = dsa_attention(hn, attn_w_in[j], attn_q_norm[j], attn_k_norm[j], attn_w_out[j])
        else:
            mix = gated_deltanet(hn, dn_w_in[j], dn_conv[j], dn_a_log[j], dn_dt_bias[j], dn_norm[j], dn_w_out[j])
        h = h + mix
        h = h + peer_ffn(rmsnorm(h, norm_ffn[i]), peer_w_q[i], peer_keys[i], peer_u[i], peer_v[i])
        gate = jax.nn.sigmoid((rmsnorm(h, norm_ple[i]) @ ple_w_gate[i]).astype(F32))
        h = h + (gate * (p[i] @ ple_w_in[i]).astype(F32)).astype(h.dtype)
    return h
```

```python
import functools

import jax
import jax.numpy as jnp
import numpy as np
from jax import lax
from jax.experimental import pallas as pl
from jax.experimental.pallas import tpu as pltpu

F32 = jnp.float32
BF16 = jnp.bfloat16
I32 = jnp.int32

RMS_EPS = 1e-6
ROPE_THETA = 500000.0
ROT_FRACTION = 4

N_HEADS = 16
N_KV_HEADS = 4
HEAD_DIM = 128
IDX_HEADS = 16
IDX_DIM = 64
INDEX_TOPK = 256

DN_QK_HEADS = 16
DN_V_HEADS = 32
DN_HEAD_DIM = 128
DN_CONV_WIDTH = 4
DN_CHUNK = 64

PEER_HEADS = 8
PEER_NKEYS = 128
PEER_TOPK = 16

LANES = 128
VMEM_LIMIT = 56 * 1024 * 1024

INT_MIN = -(2 ** 31)
NEG_BIG = -1e30

_NT = (((1,), (1,)), ((), ()))


def _cparams(*sem):
    return pltpu.CompilerParams(dimension_semantics=sem, vmem_limit_bytes=VMEM_LIMIT)


def _norm_mm_body(x_ref, g_ref, w_ref, o_ref, xn_ref):
    @pl.when(pl.program_id(1) == 0)
    def _():
        x = x_ref[...]
        ms = jnp.mean(x * x, axis=-1, keepdims=True)
        xn_ref[...] = (x * lax.rsqrt(ms + RMS_EPS) * g_ref[...]).astype(BF16)

    o_ref[...] = jnp.dot(xn_ref[...], w_ref[...], preferred_element_type=F32).astype(o_ref.dtype)


def norm_matmul(x, gain, w, *, tm, tn, out_dtype=F32):
    m, k = x.shape
    n = w.shape[1]
    return pl.pallas_call(
        _norm_mm_body,
        grid=(m // tm, n // tn),
        in_specs=[
            pl.BlockSpec((tm, k), lambda i, j: (i, 0)),
            pl.BlockSpec((1, k), lambda i, j: (0, 0)),
            pl.BlockSpec((k, tn), lambda i, j: (0, j)),
        ],
        out_specs=pl.BlockSpec((tm, tn), lambda i, j: (i, j)),
        out_shape=jax.ShapeDtypeStruct((m, n), out_dtype),
        scratch_shapes=[pltpu.VMEM((tm, k), BF16)],
        compiler_params=_cparams("parallel", "arbitrary"),
        name="norm_matmul",
    )(x, gain.reshape(1, k), w)


def _mm_res_body(x_ref, w_ref, r_ref, o_ref):
    o_ref[...] = r_ref[...] + jnp.dot(x_ref[...], w_ref[...], preferred_element_type=F32)


def matmul_residual(x, w, res, *, tm, tn):
    m, k = x.shape
    n = w.shape[1]
    return pl.pallas_call(
        _mm_res_body,
        grid=(m // tm, n // tn),
        in_specs=[
            pl.BlockSpec((tm, k), lambda i, j: (i, 0)),
            pl.BlockSpec((k, tn), lambda i, j: (0, j)),
            pl.BlockSpec((tm, tn), lambda i, j: (i, j)),
        ],
        out_specs=pl.BlockSpec((tm, tn), lambda i, j: (i, j)),
        out_shape=jax.ShapeDtypeStruct((m, n), F32),
        compiler_params=_cparams("parallel", "arbitrary"),
        name="matmul_residual",
    )(x, w, res)


def _rope_tables(t, head_dim):
    rot = head_dim // ROT_FRACTION
    half = rot // 2
    inv_freq = ROPE_THETA ** (-jnp.arange(half, dtype=F32) * (2.0 / rot))
    ang = jnp.arange(t, dtype=jnp.int32).astype(F32)[:, None] * inv_freq[None, :]
    cos, sin = jnp.cos(ang), jnp.sin(ang)
    rest = head_dim - rot
    c = jnp.concatenate([cos, cos, jnp.ones((t, rest), F32)], axis=-1)
    s_lo = jnp.concatenate([-sin, jnp.zeros((t, half + rest), F32)], axis=-1)
    s_hi = jnp.concatenate([jnp.zeros((t, half), F32), sin, jnp.zeros((t, rest), F32)], axis=-1)
    reps = LANES // head_dim
    return tuple(jnp.tile(a, (1, reps)) for a in (c, s_lo, s_hi)), half


def _rope_tile(x, c, s_lo, s_hi, half):
    return x * c + pltpu.roll(x, LANES - half, 1) * s_lo + pltpu.roll(x, half, 1) * s_hi


def _attn_prep_body(p_ref, qg_ref, kg_ref, c128_ref, sl128_ref, sh128_ref, c64_ref, sl64_ref, sh64_ref,
                    q_ref, k_ref, vt_ref, iq_ref, ik_ref, *, half128, half64):
    c128, sl128, sh128 = c128_ref[...], sl128_ref[...], sh128_ref[...]
    c64, sl64, sh64 = c64_ref[...], sl64_ref[...], sh64_ref[...]

    def normed(x, g):
        ms = jnp.mean(x * x, axis=-1, keepdims=True)
        return x * lax.rsqrt(ms + RMS_EPS) * g

    off = 0
    for h in range(N_HEADS):
        x = normed(p_ref[:, off:off + HEAD_DIM], qg_ref[...])
        q_ref[0, h] = (_rope_tile(x, c128, sl128, sh128, half128) * (HEAD_DIM ** -0.5)).astype(BF16)
        off += HEAD_DIM
    for h in range(N_KV_HEADS):
        x = normed(p_ref[:, off:off + HEAD_DIM], kg_ref[...])
        k_ref[0, h] = _rope_tile(x, c128, sl128, sh128, half128).astype(BF16)
        off += HEAD_DIM
    for h in range(N_KV_HEADS):
        vt_ref[0, h] = p_ref[:, off:off + HEAD_DIM].T.astype(BF16)
        off += HEAD_DIM
    for j in range(IDX_HEADS * IDX_DIM // LANES):
        x = _rope_tile(p_ref[:, off:off + LANES], c64, sl64, sh64, half64) * (IDX_DIM ** -0.5)
        for u in range(LANES // IDX_DIM):
            iq_ref[0, j * (LANES // IDX_DIM) + u] = x[:, u * IDX_DIM:(u + 1) * IDX_DIM].astype(BF16)
        off += LANES
    x = _rope_tile(p_ref[:, off:off + LANES], c64, sl64, sh64, half64)
    ik_ref[0] = x[:, :IDX_DIM].astype(BF16)


def attn_prep(proj, q_gain, k_gain, b, t, *, tt):
    (c128, sl128, sh128), half128 = _rope_tables(t, HEAD_DIM)
    (c64, sl64, sh64), half64 = _rope_tables(t, IDX_DIM)
    nt = t // tt
    width = proj.shape[1]
    tab = pl.BlockSpec((tt, LANES), lambda bi, ti: (ti, 0))
    gain = pl.BlockSpec((1, HEAD_DIM), lambda bi, ti: (0, 0))
    return pl.pallas_call(
        functools.partial(_attn_prep_body, half128=half128, half64=half64),
        grid=(b, nt),
        in_specs=[pl.BlockSpec((tt, width), lambda bi, ti: (bi * nt + ti, 0)), gain, gain,
                  tab, tab, tab, tab, tab, tab],
        out_specs=[
            pl.BlockSpec((1, N_HEADS, tt, HEAD_DIM), lambda bi, ti: (bi, 0, ti, 0)),
            pl.BlockSpec((1, N_KV_HEADS, tt, HEAD_DIM), lambda bi, ti: (bi, 0, ti, 0)),
            pl.BlockSpec((1, N_KV_HEADS, HEAD_DIM, tt), lambda bi, ti: (bi, 0, 0, ti)),
            pl.BlockSpec((1, IDX_HEADS, tt, IDX_DIM), lambda bi, ti: (bi, 0, ti, 0)),
            pl.BlockSpec((1, tt, IDX_DIM), lambda bi, ti: (bi, ti, 0)),
        ],
        out_shape=[
            jax.ShapeDtypeStruct((b, N_HEADS, t, HEAD_DIM), BF16),
            jax.ShapeDtypeStruct((b, N_KV_HEADS, t, HEAD_DIM), BF16),
            jax.ShapeDtypeStruct((b, N_KV_HEADS, HEAD_DIM, t), BF16),
            jax.ShapeDtypeStruct((b, IDX_HEADS, t, IDX_DIM), BF16),
            jax.ShapeDtypeStruct((b, t, IDX_DIM), BF16),
        ],
        compiler_params=_cparams("parallel", "parallel"),
        name="attn_prep",
    )(proj, q_gain.reshape(1, HEAD_DIM), k_gain.reshape(1, HEAD_DIM), c128, sl128, sh128, c64, sl64, sh64)


def _attn_body(iq_ref, iw_ref, ik_ref, q_ref, k_ref, vt_ref, o_ref,
               keys_ref, m_ref, l_ref, acc_ref, *, qb, kc, topk):
    qi = pl.program_id(1)
    group = N_HEADS // N_KV_HEADS
    nck = ((qi + 1) * qb + kc - 1) // kc
    iq = iq_ref[0].reshape(IDX_HEADS * qb, IDX_DIM)
    iw = iw_ref[0, 0] * (IDX_HEADS ** -0.5)
    qpos = qi * qb + lax.broadcasted_iota(I32, (kc, qb), 1)
    krow = lax.broadcasted_iota(I32, (kc, qb), 0)

    def score_chunk(c, carry):
        off = pl.multiple_of(c * kc, kc)
        ikc = ik_ref[0, pl.ds(off, kc), :]
        s = lax.dot_general(ikc, iq, _NT, preferred_element_type=F32)
        s = jnp.maximum(s, 0.0) * iw
        sc = s[:, 0:qb]
        for h in range(1, IDX_HEADS):
            sc = sc + s[:, h * qb:(h + 1) * qb]
        bits = pltpu.bitcast(sc, I32)
        key = bits ^ ((bits >> 31) & 0x7FFFFFFF)
        keys_ref[c] = jnp.where(krow + off <= qpos, key, INT_MIN)
        return carry

    lax.fori_loop(0, nck, score_chunk, 0)

    def count_ge(cand):
        def body(c, acc):
            ind = jnp.where(keys_ref[c] >= cand, 1, 0)
            return acc + ind.reshape(kc // 8, 8, qb).sum(axis=0)

        acc = lax.fori_loop(0, nck, body, jnp.zeros((8, qb), I32))
        return acc.sum(axis=0, keepdims=True)

    zero = jnp.zeros((1, qb), I32)
    ans = jnp.where(count_ge(zero) >= topk, zero, INT_MIN)

    def bit_step(i, ans):
        cand = ans | lax.shift_left(jnp.int32(1), 30 - i)
        return jnp.where(count_ge(cand) >= topk, cand, ans)

    ans = lax.fori_loop(0, 31, bit_step, ans)
    tau = jnp.maximum(ans, INT_MIN + 1)

    m_ref[...] = jnp.full(m_ref.shape, NEG_BIG, F32)
    l_ref[...] = jnp.zeros(l_ref.shape, F32)
    acc_ref[...] = jnp.zeros(acc_ref.shape, F32)

    def attn_chunk(c, carry):
        off = pl.multiple_of(c * kc, kc)
        sel = keys_ref[c] >= tau
        sel = jnp.concatenate([sel] * group, axis=1)
        for n in range(N_KV_HEADS):
            qn = q_ref[0, n * group:(n + 1) * group].reshape(group * qb, HEAD_DIM)
            kn = k_ref[0, n, pl.ds(off, kc), :]
            s = lax.dot_general(kn, qn, _NT, preferred_element_type=F32)
            s = jnp.where(sel, s, NEG_BIG)
            m_old = m_ref[n]
            m_new = jnp.maximum(m_old, s.max(axis=0, keepdims=True))
            alpha = jnp.exp(m_old - m_new)
            p = jnp.where(sel, jnp.exp(s - m_new), 0.0)
            l_ref[n] = alpha * l_ref[n] + p.sum(axis=0, keepdims=True)
            vn = vt_ref[0, n, :, pl.ds(off, kc)]
            acc_ref[n] = alpha * acc_ref[n] + jnp.dot(vn, p.astype(BF16), preferred_element_type=F32)
            m_ref[n] = m_new
        return carry

    lax.fori_loop(0, nck, attn_chunk, 0)

    for n in range(N_KV_HEADS):
        on = acc_ref[n] / l_ref[n]
        for g in range(group):
            h = n * group + g
            o_ref[0, :, h * HEAD_DIM:(h + 1) * HEAD_DIM] = on[:, g * qb:(g + 1) * qb].T.astype(o_ref.dtype)


def sparse_attention(q, k, vt, iq, ik, iw, *, qb, kc):
    b, _, t, _ = q.shape
    nq = t // qb
    topk = min(INDEX_TOPK, t // 4)
    group = N_HEADS // N_KV_HEADS
    iw_rows = iw.reshape(b, nq, qb, IDX_HEADS).transpose(0, 1, 3, 2).reshape(b, nq, 1, IDX_HEADS * qb)
    return pl.pallas_call(
        functools.partial(_attn_body, qb=qb, kc=kc, topk=topk),
        grid=(b, nq),
        in_specs=[
            pl.BlockSpec((1, IDX_HEADS, qb, IDX_DIM), lambda bi, qi: (bi, 0, qi, 0)),
            pl.BlockSpec((1, 1, 1, IDX_HEADS * qb), lambda bi, qi: (bi, qi, 0, 0)),
            pl.BlockSpec((1, t, IDX_DIM), lambda bi, qi: (bi, 0, 0)),
            pl.BlockSpec((1, N_HEADS, qb, HEAD_DIM), lambda bi, qi: (bi, 0, qi, 0)),
            pl.BlockSpec((1, N_KV_HEADS, t, HEAD_DIM), lambda bi, qi: (bi, 0, 0, 0)),
            pl.BlockSpec((1, N_KV_HEADS, HEAD_DIM, t), lambda bi, qi: (bi, 0, 0, 0)),
        ],
        out_specs=pl.BlockSpec((1, qb, N_HEADS * HEAD_DIM), lambda bi, qi: (bi, qi, 0)),
        out_shape=jax.ShapeDtypeStruct((b, t, N_HEADS * HEAD_DIM), BF16),
        scratch_shapes=[
            pltpu.VMEM((t // kc, kc, qb), I32),
            pltpu.VMEM((N_KV_HEADS, 1, group * qb), F32),
            pltpu.VMEM((N_KV_HEADS, 1, group * qb), F32),
            pltpu.VMEM((N_KV_HEADS, HEAD_DIM, group * qb), F32),
        ],
        compiler_params=_cparams("parallel", "arbitrary"),
        name="sparse_attention",
    )(iq, iw_rows, ik, q, k, vt)


def _pad_cols(w, mult):
    n = w.shape[1]
    pad = (-n) % mult
    return jnp.pad(w, ((0, 0), (0, pad))) if pad else w


def dsa_mixer(h2d, b, t, norm_gain, w_in, q_gain, k_gain, w_out):
    w = _pad_cols(w_in, LANES).astype(BF16)
    proj = norm_matmul(h2d, norm_gain, w, tm=512, tn=w.shape[1] // 3)
    q, k, vt, iq, ik = attn_prep(proj, q_gain, k_gain, b, t, tt=256)
    iw_off = N_HEADS * HEAD_DIM + 2 * N_KV_HEADS * HEAD_DIM + IDX_HEADS * IDX_DIM + IDX_DIM
    iw = proj[:, iw_off:iw_off + IDX_HEADS].reshape(b, t, IDX_HEADS)
    o = sparse_attention(q, k, vt, iq, ik, iw, qb=128, kc=512)
    return matmul_residual(o.reshape(b * t, -1), w_out.astype(BF16), h2d, tm=512, tn=512)


def _peer_q_body(x_ref, g_ref, w_ref, qt_ref, xn_ref):
    @pl.when(pl.program_id(1) == 0)
    def _():
        x = x_ref[...]
        ms = jnp.mean(x * x, axis=-1, keepdims=True)
        xn_ref[...] = (x * lax.rsqrt(ms + RMS_EPS) * g_ref[...]).astype(BF16)

    qt_ref[...] = lax.dot_general(w_ref[...], xn_ref[...], _NT, preferred_element_type=F32)


def peer_query(h2d, gain, w_qt, *, tm, tn):
    m, k = h2d.shape
    n = w_qt.shape[0]
    return pl.pallas_call(
        _peer_q_body,
        grid=(m // tm, n // tn),
        in_specs=[
            pl.BlockSpec((tm, k), lambda i, j: (i, 0)),
            pl.BlockSpec((1, k), lambda i, j: (0, 0)),
            pl.BlockSpec((tn, k), lambda i, j: (j, 0)),
        ],
        out_specs=[
            pl.BlockSpec((tn, tm), lambda i, j: (j, i)),
            pl.BlockSpec((tm, k), lambda i, j: (i, 0)),
        ],
        out_shape=[jax.ShapeDtypeStruct((n, m), F32), jax.ShapeDtypeStruct((m, k), BF16)],
        compiler_params=_cparams("parallel", "arbitrary"),
        name="peer_query",
    )(h2d, gain.reshape(1, k), w_qt)


def _top_values(x, dst_ref, n):
    rank = jnp.full(x.shape, float(n + 1), F32)
    for r in range(n):
        m = x.max(axis=0, keepdims=True)
        dst_ref[r:r + 1, :] = m
        hit = x == m
        rank = jnp.where(hit, float(r + 1), rank)
        x = jnp.where(hit, -jnp.inf, x)
    return rank


def _peer_route_body(qt_ref, keys_ref, n_ref, e0_ref, rank_ref, e1_ref, a_ref, b_ref, cand_ref, f_ref):
    k = PEER_TOPK
    for h in range(PEER_HEADS):
        s, ranks = [], []
        for p, dst in ((0, a_ref), (1, b_ref)):
            r0 = (h * 2 + p) * PEER_NKEYS
            sp = jnp.dot(keys_ref[h, p], qt_ref[r0:r0 + PEER_NKEYS, :],
                         precision=lax.Precision.HIGHEST, preferred_element_type=F32)
            s.append(sp)
            ranks.append(_top_values(sp, dst, k))
        bv = b_ref[...]
        for i in range(k):
            cand_ref[i * k:(i + 1) * k, :] = a_ref[i:i + 1, :] + bv
        _top_values(cand_ref[...], f_ref, k)
        f = f_ref[...]
        z = jnp.exp(f - f[0:1, :]).sum(axis=0, keepdims=True)
        tau = f[k - 1:k, :]
        partners = jnp.zeros(s[0].shape, F32)
        for i in range(k):
            cnt = jnp.where(cand_ref[i * k:(i + 1) * k, :] >= tau, 1.0, 0.0).sum(axis=0, keepdims=True)
            partners = jnp.where(ranks[0] == float(i + 1), cnt, partners)
        n_ref[h] = partners
        e0_ref[h] = jnp.exp(s[0] - a_ref[0:1, :])
        rank_ref[h] = ranks[1]
        e1_ref[h] = jnp.exp(s[1] - b_ref[0:1, :]) / z


def peer_route(qt, keys, *, tt):
    n, m = qt.shape
    spec = pl.BlockSpec((PEER_HEADS, PEER_NKEYS, tt), lambda i: (0, 0, i))
    shp = jax.ShapeDtypeStruct((PEER_HEADS, PEER_NKEYS, m), F32)
    return pl.pallas_call(
        _peer_route_body,
        grid=(m // tt,),
        in_specs=[
            pl.BlockSpec((n, tt), lambda i: (0, i)),
            pl.BlockSpec(keys.shape, lambda i: (0, 0, 0, 0)),
        ],
        out_specs=[spec, spec, spec, spec],
        out_shape=[shp, shp, shp, shp],
        scratch_shapes=[
            pltpu.VMEM((PEER_TOPK, tt), F32),
            pltpu.VMEM((PEER_TOPK, tt), F32),
            pltpu.VMEM((PEER_TOPK * PEER_TOPK, tt), F32),
            pltpu.VMEM((PEER_TOPK, tt), F32),
        ],
        compiler_params=_cparams("parallel"),
        name="peer_route",
    )(qt, keys)


def _gelu(x):
    return 0.5 * x * (1.0 + lax.erf(x * (2.0 ** -0.5)))


def _peer_expert_body(xn_ref, u_ref, vt_ref, n_ref, e0_ref, rank_ref, e1_ref, h_ref, o_ref, acc_ref, coef_ref, *, rows):
    e = pl.program_id(1)

    @pl.when(e == 0)
    def _():
        acc_ref[...] = jnp.zeros(acc_ref.shape, F32)

    act = lax.dot_general(u_ref[...], xn_ref[...], _NT, preferred_element_type=F32)
    for ii in range(rows):
        w = None
        for hd in range(PEER_HEADS):
            npart = n_ref[hd, ii:ii + 1, :]
            e0 = e0_ref[hd, ii:ii + 1, :]
            t = jnp.where(rank_ref[hd] <= npart, e1_ref[hd], 0.0) * e0
            w = t if w is None else w + t
        sl = slice(ii * PEER_NKEYS, (ii + 1) * PEER_NKEYS)
        coef_ref[sl, :] = (w * _gelu(act[sl, :])).astype(BF16)
    acc_ref[...] += jnp.dot(vt_ref[...], coef_ref[...], preferred_element_type=F32)

    @pl.when(e == pl.num_programs(1) - 1)
    def _():
        o_ref[...] = h_ref[...] + acc_ref[...].T


def peer_experts(xn, u, vt, npart, e0, rank, e1, h2d, *, tt, te):
    m, d = xn.shape
    n_exp = u.shape[0]
    rows = te // PEER_NKEYS
    row_spec = pl.BlockSpec((PEER_HEADS, rows, tt), lambda i, e: (0, e, i))
    col_spec = pl.BlockSpec((PEER_HEADS, PEER_NKEYS, tt), lambda i, e: (0, 0, i))
    tok_spec = pl.BlockSpec((tt, d), lambda i, e: (i, 0))
    return pl.pallas_call(
        functools.partial(_peer_expert_body, rows=rows),
        grid=(m // tt, n_exp // te),
        in_specs=[
            tok_spec,
            pl.BlockSpec((te, d), lambda i, e: (e, 0)),
            pl.BlockSpec((d, te), lambda i, e: (0, e)),
            row_spec, row_spec, col_spec, col_spec,
            tok_spec,
        ],
        out_specs=tok_spec,
        out_shape=jax.ShapeDtypeStruct((m, d), F32),
        scratch_shapes=[pltpu.VMEM((d, tt), F32), pltpu.VMEM((te, tt), BF16)],
        compiler_params=_cparams("parallel", "arbitrary"),
        name="peer_experts",
    )(xn, u, vt, npart, e0, rank, e1, h2d)


def peer_mixer(h2d, gain, w_q, keys, u_tab, v_tab):
    qt, xn = peer_query(h2d, gain, w_q.T.astype(BF16), tm=512, tn=512)
    npart, e0, rank, e1 = peer_route(qt, keys, tt=256)
    return peer_experts(xn, u_tab.astype(BF16), v_tab.T.astype(BF16), npart, e0, rank, e1, h2d, tt=512, te=1024)


def _ple_body(x_ref, g_ref, wg_ref, p_ref, wi_ref, r_ref, o_ref, xn_ref):
    @pl.when(pl.program_id(1) == 0)
    def _():
        x = x_ref[...]
        ms = jnp.mean(x * x, axis=-1, keepdims=True)
        xn_ref[...] = (x * lax.rsqrt(ms + RMS_EPS) * g_ref[...]).astype(BF16)

    gate = jax.nn.sigmoid(jnp.dot(xn_ref[...], wg_ref[...], preferred_element_type=F32))
    emb = jnp.dot(p_ref[...], wi_ref[...], preferred_element_type=F32)
    o_ref[...] = r_ref[...] + gate * emb


def ple_mixer(h2d, gain, w_gate, p2d, w_in, *, tm, tn):
    m, d = h2d.shape
    pd = p2d.shape[1]
    return pl.pallas_call(
        _ple_body,
        grid=(m // tm, d // tn),
        in_specs=[
            pl.BlockSpec((tm, d), lambda i, j: (i, 0)),
            pl.BlockSpec((1, d), lambda i, j: (0, 0)),
            pl.BlockSpec((d, tn), lambda i, j: (0, j)),
            pl.BlockSpec((tm, pd), lambda i, j: (i, 0)),
            pl.BlockSpec((pd, tn), lambda i, j: (0, j)),
            pl.BlockSpec((tm, tn), lambda i, j: (i, j)),
        ],
        out_specs=pl.BlockSpec((tm, tn), lambda i, j: (i, j)),
        out_shape=jax.ShapeDtypeStruct((m, d), F32),
        scratch_shapes=[pltpu.VMEM((tm, d), BF16)],
        compiler_params=_cparams("parallel", "arbitrary"),
        name="ple_mixer",
    )(h2d, gain.reshape(1, d), w_gate.astype(BF16), p2d.astype(BF16), w_in.astype(BF16), h2d)


def _dn_conv_body(x_ref, halo_ref, w_ref, *rest, mode, tt):
    if mode == "k":
        o_ref, ot_ref, ext_ref = rest
    else:
        o_ref, ext_ref = rest
    width = DN_CONV_WIDTH
    first = pl.program_id(1) == 0
    ext_ref[0:8, :] = jnp.where(first, 0.0, halo_ref[...])
    ext_ref[8:, :] = x_ref[...]
    y = None
    for j in range(width):
        term = w_ref[j:j + 1, :] * ext_ref[pl.ds(8 - (width - 1) + j, tt), :]
        y = term if y is None else y + term
    y = y * jax.nn.sigmoid(y)
    if mode in ("q", "k"):
        y = y * lax.rsqrt(jnp.sum(y * y, axis=-1, keepdims=True) + RMS_EPS)
    if mode == "q":
        y = y * (DN_HEAD_DIM ** -0.5)
    o_ref[0, 0] = y.astype(o_ref.dtype)
    if mode == "k":
        ot_ref[0, 0] = y.T.astype(ot_ref.dtype)


def dn_conv_silu(proj, conv_w, b, t, *, mode, col0, heads, tt):
    nt = t // tt
    cb0 = col0 // LANES
    out_spec = pl.BlockSpec((1, 1, tt, DN_HEAD_DIM), lambda bi, ti, hi: (bi, hi, ti, 0))
    out_shape = jax.ShapeDtypeStruct((b, heads, t, DN_HEAD_DIM), F32)
    out_specs, out_shapes = [out_spec], [out_shape]
    if mode == "k":
        out_specs.append(pl.BlockSpec((1, 1, DN_HEAD_DIM, tt), lambda bi, ti, hi: (bi, hi, 0, ti)))
        out_shapes.append(jax.ShapeDtypeStruct((b, heads, DN_HEAD_DIM, t), F32))
    res = pl.pallas_call(
        functools.partial(_dn_conv_body, mode=mode, tt=tt),
        grid=(b, nt, heads),
        in_specs=[
            pl.BlockSpec((tt, LANES), lambda bi, ti, hi: (bi * nt + ti, cb0 + hi)),
            pl.BlockSpec((8, LANES), lambda bi, ti, hi: (jnp.maximum((bi * nt + ti) * (tt // 8) - 1, 0), cb0 + hi)),
            pl.BlockSpec((DN_CONV_WIDTH, LANES), lambda bi, ti, hi: (0, cb0 + hi)),
        ],
        out_specs=out_specs,
        out_shape=out_shapes,
        scratch_shapes=[pltpu.VMEM((tt + 8, LANES), F32)],
        compiler_params=_cparams("parallel", "parallel", "parallel"),
        name="dn_conv_" + mode,
    )(proj, proj, conv_w)
    return res if mode == "k" else res[0]


def _dn_gate_body(ba_ref, alog_ref, dtb_ref, beta_ref, gcum_ref, *, tm):
    nh = DN_V_HEADS
    beta_ref[...] = jax.nn.sigmoid(ba_ref[0:nh, :])
    a = ba_ref[nh:2 * nh, :]
    g = -jnp.exp(alog_ref[...]) * jax.nn.softplus(a + dtb_ref[...])
    lane = lax.broadcasted_iota(I32, (nh, LANES), 1) % DN_CHUNK
    for j in range(tm // LANES):
        x = g[:, j * LANES:(j + 1) * LANES]
        s = 1
        while s < DN_CHUNK:
            x = x + jnp.where(lane >= s, pltpu.roll(x, s, 1), 0.0)
            s *= 2
        gcum_ref[:, j * LANES:(j + 1) * LANES] = x


def dn_gates(ba_t, a_log, dt_bias, *, tm):
    m = ba_t.shape[1]
    nh = DN_V_HEADS
    spec = pl.BlockSpec((nh, tm), lambda i: (0, i))
    col = pl.BlockSpec((nh, 1), lambda i: (0, 0))
    return pl.pallas_call(
        functools.partial(_dn_gate_body, tm=tm),
        grid=(m // tm,),
        in_specs=[pl.BlockSpec((ba_t.shape[0], tm), lambda i: (0, i)), col, col],
        out_specs=[spec, spec],
        out_shape=[jax.ShapeDtypeStruct((nh, m), F32)] * 2,
        compiler_params=_cparams("parallel"),
        name="dn_gates",
    )(ba_t, a_log.reshape(nh, 1), dt_bias.reshape(nh, 1))


def _unit_lower_inverse(a, eye):
    c = a.shape[0]
    inv = eye - a
    x = a
    span = 2
    while span < c:
        x = jnp.dot(x.astype(BF16), x.astype(BF16), preferred_element_type=F32)
        inv = inv + jnp.dot(inv.astype(BF16), x.astype(BF16), preferred_element_type=F32)
        span *= 2
    return inv


def _dn_delta_body(q_ref, k_ref, kt_ref, v_ref, z_ref, gc_ref, gr_ref, bc_ref, gain_ref, o_ref, state_ref, *, gsz, nchunk):
    c, d = DN_CHUNK, DN_HEAD_DIM
    rep = DN_V_HEADS // DN_QK_HEADS

    @pl.when(pl.program_id(2) == 0)
    def _():
        state_ref[...] = jnp.zeros(state_ref.shape, F32)

    row = lax.broadcasted_iota(I32, (c, c), 0)
    colm = lax.broadcasted_iota(I32, (c, c), 1)
    tril, strict = row >= colm, row > colm
    eye = jnp.where(row == colm, 1.0, 0.0)
    gain = gain_ref[...]
    for cc in range(nchunk):
        rows = slice(cc * c, (cc + 1) * c)
        for jq in range(gsz // rep):
            q = q_ref[0, jq, rows, :]
            k = k_ref[0, jq, rows, :]
            kt = kt_ref[0, jq, :, rows]
            kk = lax.dot_general(k.astype(BF16), k.astype(BF16), _NT, preferred_element_type=F32)
            qk = lax.dot_general(q.astype(BF16), k.astype(BF16), _NT, preferred_element_type=F32)
            for jv in range(jq * rep, (jq + 1) * rep):
                gc = gc_ref[0, 0, rows, jv:jv + 1]
                gr = gr_ref[0, 0, jv:jv + 1, rows]
                beta = bc_ref[0, 0, rows, jv:jv + 1]
                decay = jnp.exp(jnp.where(tril, gc - gr, NEG_BIG))
                a = jnp.where(strict, kk * decay, 0.0) * beta
                inv = _unit_lower_inverse(a, eye)
                v = v_ref[0, jv, rows, :]
                eg = jnp.exp(gc)
                rhs = jnp.concatenate([v * beta, k * (beta * eg)], axis=1)
                sol = jnp.dot(inv.astype(BF16), rhs.astype(BF16), preferred_element_type=F32)
                u, w = sol[:, :d], sol[:, d:]
                s_old = state_ref[jv]
                r1 = jnp.dot(jnp.concatenate([w, q * eg], axis=0).astype(BF16), s_old.astype(BF16),
                             preferred_element_type=F32)
                v_new = u - r1[:c]
                g_last = gc[c - 1:c, :]
                kd_t = kt * jnp.exp(g_last - gr)
                r2 = jnp.dot(jnp.concatenate([qk * decay, kd_t], axis=0).astype(BF16), v_new.astype(BF16),
                             preferred_element_type=F32)
                o = r1[c:] + r2[:c]
                state_ref[jv] = s_old * jnp.exp(g_last) + r2[c:]
                ms = jnp.mean(o * o, axis=-1, keepdims=True)
                z = z_ref[rows, jv * d:(jv + 1) * d]
                o_ref[rows, jv * d:(jv + 1) * d] = (
                    o * lax.rsqrt(ms + RMS_EPS) * gain * (z * jax.nn.sigmoid(z))).astype(o_ref.dtype)


def dn_delta(q, k, kt, v, proj, z_col0, gc_col, gc_row, beta_col, norm_gain, *, gsz, tb):
    b, hv, t, d = v.shape
    rep = DN_V_HEADS // DN_QK_HEADS
    hg = hv // gsz
    nb = t // tb
    zb0 = z_col0 // (gsz * d)
    return pl.pallas_call(
        functools.partial(_dn_delta_body, gsz=gsz, nchunk=tb // DN_CHUNK),
        grid=(b, hg, nb),
        in_specs=[
            pl.BlockSpec((1, gsz // rep, tb, d), lambda bi, gi, ni: (bi, gi, ni, 0)),
            pl.BlockSpec((1, gsz // rep, tb, d), lambda bi, gi, ni: (bi, gi, ni, 0)),
            pl.BlockSpec((1, gsz // rep, d, tb), lambda bi, gi, ni: (bi, gi, 0, ni)),
            pl.BlockSpec((1, gsz, tb, d), lambda bi, gi, ni: (bi, gi, ni, 0)),
            pl.BlockSpec((tb, gsz * d), lambda bi, gi, ni: (bi * nb + ni, zb0 + gi)),
            pl.BlockSpec((1, 1, tb, gsz), lambda bi, gi, ni: (bi, gi, ni, 0)),
            pl.BlockSpec((1, 1, gsz, tb), lambda bi, gi, ni: (bi, gi, 0, ni)),
            pl.BlockSpec((1, 1, tb, gsz), lambda bi, gi, ni: (bi, gi, ni, 0)),
            pl.BlockSpec((1, d), lambda bi, gi, ni: (0, 0)),
        ],
        out_specs=pl.BlockSpec((tb, gsz * d), lambda bi, gi, ni: (bi * nb + ni, gi)),
        out_shape=jax.ShapeDtypeStruct((b * t, hv * d), BF16),
        scratch_shapes=[pltpu.VMEM((gsz, d, d), F32)],
        compiler_params=_cparams("parallel", "parallel", "arbitrary"),
        name="dn_delta",
    )(q, k, kt, v, proj, gc_col, gc_row, beta_col, norm_gain.reshape(1, d))


def gdn_mixer(h2d, b, t, norm_gain, w_in, conv_w, a_log, dt_bias, out_gain, w_out):
    key_dim = DN_QK_HEADS * DN_HEAD_DIM
    val_dim = DN_V_HEADS * DN_HEAD_DIM
    main = 2 * key_dim + 2 * val_dim
    proj = norm_matmul(h2d, norm_gain, w_in[:, :main].astype(BF16), tm=512, tn=1024)
    w_ba_t = _pad_cols(w_in[:, main:], LANES).T.astype(BF16)
    ba_t, _ = peer_query(h2d, norm_gain, w_ba_t, tm=512, tn=LANES)
    beta_t, gcum_t = dn_gates(ba_t, a_log, dt_bias, tm=512)
    q = dn_conv_silu(proj, conv_w, b, t, mode="q", col0=0, heads=DN_QK_HEADS, tt=512)
    k, kt = dn_conv_silu(proj, conv_w, b, t, mode="k", col0=key_dim, heads=DN_QK_HEADS, tt=512)
    v = dn_conv_silu(proj, conv_w, b, t, mode="v", col0=2 * key_dim, heads=DN_V_HEADS, tt=512)
    gsz = 8
    hg = DN_V_HEADS // gsz
    rows = lambda x: x.reshape(hg, gsz, b, t).transpose(2, 0, 1, 3)
    cols = lambda x: rows(x).transpose(0, 1, 3, 2)
    o = dn_delta(q, k, kt, v, proj, 2 * key_dim + val_dim, cols(gcum_t), rows(gcum_t), cols(beta_t), out_gain,
                 gsz=gsz, tb=2 * DN_CHUNK)
    return matmul_residual(o, w_out.astype(BF16), h2d, tm=512, tn=512)


def kernel(x, p, norm_mix, norm_ffn, norm_ple, attn_w_in, attn_q_norm, attn_k_norm, attn_w_out, dn_w_in, dn_conv,
           dn_a_log, dn_dt_bias, dn_norm, dn_w_out, peer_w_q, peer_keys, peer_u, peer_v, ple_w_in, ple_w_gate):
    b, t, d = x.shape
    n_mixers = 2
    h = x.reshape(b * t, d)
    for i in range(p.shape[0]):
        j = i // n_mixers
        if i % n_mixers == 0:
            h = dsa_mixer(h, b, t, norm_mix[i], attn_w_in[j], attn_q_norm[j], attn_k_norm[j], attn_w_out[j])
        else:
            h = gdn_mixer(h, b, t, norm_mix[i], dn_w_in[j], dn_conv[j], dn_a_log[j], dn_dt_bias[j], dn_norm[j],
                          dn_w_out[j])
        h = peer_mixer(h, norm_ffn[i], peer_w_q[i], peer_keys[i], peer_u[i], peer_v[i])
        h = ple_mixer(h, norm_ple[i], ple_w_gate[i], p[i].reshape(b * t, -1), ple_w_in[i], tm=512, tn=512)
    return h.reshape(b, t, d)
```

```python
import functools

import jax
import jax.numpy as jnp
import numpy as np
from jax import lax
from jax.experimental import pallas as pl
from jax.experimental.pallas import tpu as pltpu

F32 = jnp.float32
BF16 = jnp.bfloat16
I32 = jnp.int32

RMS_EPS = 1e-6
ROPE_THETA = 500000.0
ROT_FRACTION = 4

N_HEADS = 16
N_KV_HEADS = 4
HEAD_DIM = 128
IDX_HEADS = 16
IDX_DIM = 64
INDEX_TOPK = 256

DN_QK_HEADS = 16
DN_V_HEADS = 32
DN_HEAD_DIM = 128
DN_CONV_WIDTH = 4
DN_CHUNK = 64

PEER_HEADS = 8
PEER_NKEYS = 128
PEER_TOPK = 16

LANES = 128
VMEM_LIMIT = 56 * 1024 * 1024

LOG2E = 1.4426950408889634
INT_MIN = -(2 ** 31)
NEG_BIG = -1e30

_NT = (((1,), (1,)), ((), ()))


def _cparams(*sem):
    return pltpu.CompilerParams(dimension_semantics=sem, vmem_limit_bytes=VMEM_LIMIT)


def _norm_mm_body(x_ref, g_ref, w_ref, o_ref, xn_ref):
    @pl.when(pl.program_id(1) == 0)
    def _():
        x = x_ref[...]
        ms = jnp.mean(x * x, axis=-1, keepdims=True)
        xn_ref[...] = (x * lax.rsqrt(ms + RMS_EPS) * g_ref[...]).astype(BF16)

    o_ref[...] = jnp.dot(xn_ref[...], w_ref[...], preferred_element_type=F32).astype(o_ref.dtype)


def norm_matmul(x, gain, w, *, tm, tn, out_dtype=F32):
    m, k = x.shape
    n = w.shape[1]
    return pl.pallas_call(
        _norm_mm_body,
        grid=(m // tm, n // tn),
        in_specs=[
            pl.BlockSpec((tm, k), lambda i, j: (i, 0)),
            pl.BlockSpec((1, k), lambda i, j: (0, 0)),
            pl.BlockSpec((k, tn), lambda i, j: (0, j)),
        ],
        out_specs=pl.BlockSpec((tm, tn), lambda i, j: (i, j)),
        out_shape=jax.ShapeDtypeStruct((m, n), out_dtype),
        scratch_shapes=[pltpu.VMEM((tm, k), BF16)],
        compiler_params=_cparams("parallel", "arbitrary"),
        name="norm_matmul",
    )(x, gain.reshape(1, k), w)


def _mm_res_body(x_ref, w_ref, r_ref, o_ref):
    o_ref[...] = r_ref[...] + jnp.dot(x_ref[...], w_ref[...], preferred_element_type=F32)


def matmul_residual(x, w, res, *, tm, tn):
    m, k = x.shape
    n = w.shape[1]
    return pl.pallas_call(
        _mm_res_body,
        grid=(m // tm, n // tn),
        in_specs=[
            pl.BlockSpec((tm, k), lambda i, j: (i, 0)),
            pl.BlockSpec((k, tn), lambda i, j: (0, j)),
            pl.BlockSpec((tm, tn), lambda i, j: (i, j)),
        ],
        out_specs=pl.BlockSpec((tm, tn), lambda i, j: (i, j)),
        out_shape=jax.ShapeDtypeStruct((m, n), F32),
        compiler_params=_cparams("parallel", "arbitrary"),
        name="matmul_residual",
    )(x, w, res)


def _rope_tables(t, head_dim):
    rot = head_dim // ROT_FRACTION
    half = rot // 2
    inv_freq = ROPE_THETA ** (-jnp.arange(half, dtype=F32) * (2.0 / rot))
    ang = jnp.arange(t, dtype=jnp.int32).astype(F32)[:, None] * inv_freq[None, :]
    cos, sin = jnp.cos(ang), jnp.sin(ang)
    rest = head_dim - rot
    c = jnp.concatenate([cos, cos, jnp.ones((t, rest), F32)], axis=-1)
    s_lo = jnp.concatenate([-sin, jnp.zeros((t, half + rest), F32)], axis=-1)
    s_hi = jnp.concatenate([jnp.zeros((t, half), F32), sin, jnp.zeros((t, rest), F32)], axis=-1)
    reps = LANES // head_dim
    return tuple(jnp.tile(a, (1, reps)) for a in (c, s_lo, s_hi)), half


def _rope_tile(x, c, s_lo, s_hi, half):
    return x * c + pltpu.roll(x, LANES - half, 1) * s_lo + pltpu.roll(x, half, 1) * s_hi


def _attn_prep_body(p_ref, qg_ref, kg_ref, c128_ref, sl128_ref, sh128_ref, c64_ref, sl64_ref, sh64_ref,
                    q_ref, k_ref, vt_ref, iq_ref, ik_ref, *, half128, half64):
    c128, sl128, sh128 = c128_ref[...], sl128_ref[...], sh128_ref[...]
    c64, sl64, sh64 = c64_ref[...], sl64_ref[...], sh64_ref[...]

    def normed(x, g):
        ms = jnp.mean(x * x, axis=-1, keepdims=True)
        return x * lax.rsqrt(ms + RMS_EPS) * g

    off = 0
    for h in range(N_HEADS):
        x = normed(p_ref[:, off:off + HEAD_DIM], qg_ref[...])
        q_ref[0, h] = (_rope_tile(x, c128, sl128, sh128, half128) * (HEAD_DIM ** -0.5 * LOG2E)).astype(BF16)
        off += HEAD_DIM
    for h in range(N_KV_HEADS):
        x = normed(p_ref[:, off:off + HEAD_DIM], kg_ref[...])
        k_ref[0, h] = _rope_tile(x, c128, sl128, sh128, half128).astype(BF16)
        off += HEAD_DIM
    for h in range(N_KV_HEADS):
        vt_ref[0, h] = p_ref[:, off:off + HEAD_DIM].T.astype(BF16)
        off += HEAD_DIM
    for j in range(IDX_HEADS * IDX_DIM // LANES):
        x = _rope_tile(p_ref[:, off:off + LANES], c64, sl64, sh64, half64) * (IDX_DIM ** -0.5)
        for u in range(LANES // IDX_DIM):
            iq_ref[0, j * (LANES // IDX_DIM) + u] = x[:, u * IDX_DIM:(u + 1) * IDX_DIM].astype(BF16)
        off += LANES
    x = _rope_tile(p_ref[:, off:off + LANES], c64, sl64, sh64, half64)
    ik_ref[0] = x[:, :IDX_DIM].astype(BF16)


def attn_prep(proj, q_gain, k_gain, b, t, *, tt):
    (c128, sl128, sh128), half128 = _rope_tables(t, HEAD_DIM)
    (c64, sl64, sh64), half64 = _rope_tables(t, IDX_DIM)
    nt = t // tt
    width = proj.shape[1]
    tab = pl.BlockSpec((tt, LANES), lambda bi, ti: (ti, 0))
    gain = pl.BlockSpec((1, HEAD_DIM), lambda bi, ti: (0, 0))
    return pl.pallas_call(
        functools.partial(_attn_prep_body, half128=half128, half64=half64),
        grid=(b, nt),
        in_specs=[pl.BlockSpec((tt, width), lambda bi, ti: (bi * nt + ti, 0)), gain, gain,
                  tab, tab, tab, tab, tab, tab],
        out_specs=[
            pl.BlockSpec((1, N_HEADS, tt, HEAD_DIM), lambda bi, ti: (bi, 0, ti, 0)),
            pl.BlockSpec((1, N_KV_HEADS, tt, HEAD_DIM), lambda bi, ti: (bi, 0, ti, 0)),
            pl.BlockSpec((1, N_KV_HEADS, HEAD_DIM, tt), lambda bi, ti: (bi, 0, 0, ti)),
            pl.BlockSpec((1, IDX_HEADS, tt, IDX_DIM), lambda bi, ti: (bi, 0, ti, 0)),
            pl.BlockSpec((1, tt, IDX_DIM), lambda bi, ti: (bi, ti, 0)),
        ],
        out_shape=[
            jax.ShapeDtypeStruct((b, N_HEADS, t, HEAD_DIM), BF16),
            jax.ShapeDtypeStruct((b, N_KV_HEADS, t, HEAD_DIM), BF16),
            jax.ShapeDtypeStruct((b, N_KV_HEADS, HEAD_DIM, t), BF16),
            jax.ShapeDtypeStruct((b, IDX_HEADS, t, IDX_DIM), BF16),
            jax.ShapeDtypeStruct((b, t, IDX_DIM), BF16),
        ],
        compiler_params=_cparams("parallel", "parallel"),
        name="attn_prep",
    )(proj, q_gain.reshape(1, HEAD_DIM), k_gain.reshape(1, HEAD_DIM), c128, sl128, sh128, c64, sl64, sh64)


def _attn_body(iq_ref, iw_ref, ik_ref, q_ref, k_ref, vt_ref, o_ref,
               keys_ref, m_ref, l_ref, acc_ref, *, qb, kc, topk):
    qi = pl.program_id(1)
    group = N_HEADS // N_KV_HEADS
    nck = ((qi + 1) * qb + kc - 1) // kc
    iq = iq_ref[0].reshape(IDX_HEADS * qb, IDX_DIM)
    iw = iw_ref[0, 0] * (IDX_HEADS ** -0.5)
    qpos = qi * qb + lax.broadcasted_iota(I32, (kc, qb), 1)
    krow = lax.broadcasted_iota(I32, (kc, qb), 0)

    def score_chunk(c, carry):
        off = pl.multiple_of(c * kc, kc)
        ikc = ik_ref[0, pl.ds(off, kc), :]
        s = lax.dot_general(ikc, iq, _NT, preferred_element_type=F32)
        s = jnp.maximum(s, 0.0) * iw
        sc = s[:, 0:qb]
        for h in range(1, IDX_HEADS):
            sc = sc + s[:, h * qb:(h + 1) * qb]
        bits = pltpu.bitcast(sc, I32)
        key = bits ^ ((bits >> 31) & 0x7FFFFFFF)
        keys_ref[c] = jnp.where(krow + off <= qpos, key, INT_MIN)
        return carry

    lax.fori_loop(0, nck, score_chunk, 0)

    def count_ge(cand):
        def body(c, acc):
            ind = jnp.where(keys_ref[c] >= cand, 1, 0)
            return acc + ind.reshape(kc // 8, 8, qb).sum(axis=0)

        acc = lax.fori_loop(0, nck, body, jnp.zeros((8, qb), I32))
        return acc.sum(axis=0, keepdims=True)

    zero = jnp.zeros((1, qb), I32)
    ans = jnp.where(count_ge(zero) >= topk, zero, INT_MIN)

    def bit_step(i, ans):
        cand = ans | lax.shift_left(jnp.int32(1), 30 - i)
        return jnp.where(count_ge(cand) >= topk, cand, ans)

    ans = lax.fori_loop(0, 31, bit_step, ans)
    tau = jnp.maximum(ans, INT_MIN + 1)

    m_ref[...] = jnp.full(m_ref.shape, NEG_BIG, F32)
    l_ref[...] = jnp.zeros(l_ref.shape, F32)
    acc_ref[...] = jnp.zeros(acc_ref.shape, F32)

    def attn_chunk(c, carry):
        off = pl.multiple_of(c * kc, kc)
        bias = jnp.where(keys_ref[c] >= tau, 0.0, NEG_BIG)
        bias = jnp.concatenate([bias] * group, axis=1)
        heads = range(N_KV_HEADS)
        s = [lax.dot_general(k_ref[0, n, pl.ds(off, kc), :],
                             q_ref[0, n * group:(n + 1) * group].reshape(group * qb, HEAD_DIM),
                             _NT, preferred_element_type=F32) + bias for n in heads]
        m_new = [jnp.maximum(m_ref[n], s[n].max(axis=0, keepdims=True)) for n in heads]
        p = [jnp.exp2(s[n] - m_new[n]) for n in heads]
        pv = [jnp.dot(vt_ref[0, n, :, pl.ds(off, kc)], p[n].astype(BF16), preferred_element_type=F32)
              for n in heads]
        for n in heads:
            alpha = jnp.exp2(m_ref[n] - m_new[n])
            l_ref[n] = alpha * l_ref[n] + p[n].sum(axis=0, keepdims=True)
            acc_ref[n] = alpha * acc_ref[n] + pv[n]
            m_ref[n] = m_new[n]
        return carry

    lax.fori_loop(0, nck, attn_chunk, 0)

    for n in range(N_KV_HEADS):
        on = acc_ref[n] / l_ref[n]
        for g in range(group):
            h = n * group + g
            o_ref[0, :, h * HEAD_DIM:(h + 1) * HEAD_DIM] = on[:, g * qb:(g + 1) * qb].T.astype(o_ref.dtype)


def sparse_attention(q, k, vt, iq, ik, iw, *, qb, kc):
    b, _, t, _ = q.shape
    nq = t // qb
    topk = min(INDEX_TOPK, t // 4)
    group = N_HEADS // N_KV_HEADS
    iw_rows = iw.reshape(b, nq, qb, IDX_HEADS).transpose(0, 1, 3, 2).reshape(b, nq, 1, IDX_HEADS * qb)
    return pl.pallas_call(
        functools.partial(_attn_body, qb=qb, kc=kc, topk=topk),
        grid=(b, nq),
        in_specs=[
            pl.BlockSpec((1, IDX_HEADS, qb, IDX_DIM), lambda bi, qi: (bi, 0, qi, 0)),
            pl.BlockSpec((1, 1, 1, IDX_HEADS * qb), lambda bi, qi: (bi, qi, 0, 0)),
            pl.BlockSpec((1, t, IDX_DIM), lambda bi, qi: (bi, 0, 0)),
            pl.BlockSpec((1, N_HEADS, qb, HEAD_DIM), lambda bi, qi: (bi, 0, qi, 0)),
            pl.BlockSpec((1, N_KV_HEADS, t, HEAD_DIM), lambda bi, qi: (bi, 0, 0, 0)),
            pl.BlockSpec((1, N_KV_HEADS, HEAD_DIM, t), lambda bi, qi: (bi, 0, 0, 0)),
        ],
        out_specs=pl.BlockSpec((1, qb, N_HEADS * HEAD_DIM), lambda bi, qi: (bi, qi, 0)),
        out_shape=jax.ShapeDtypeStruct((b, t, N_HEADS * HEAD_DIM), BF16),
        scratch_shapes=[
            pltpu.VMEM((t // kc, kc, qb), I32),
            pltpu.VMEM((N_KV_HEADS, 1, group * qb), F32),
            pltpu.VMEM((N_KV_HEADS, 1, group * qb), F32),
            pltpu.VMEM((N_KV_HEADS, HEAD_DIM, group * qb), F32),
        ],
        compiler_params=_cparams("parallel", "arbitrary"),
        name="sparse_attention",
    )(iq, iw_rows, ik, q, k, vt)


def _pad_cols(w, mult):
    n = w.shape[1]
    pad = (-n) % mult
    return jnp.pad(w, ((0, 0), (0, pad))) if pad else w


def dsa_mixer(h2d, b, t, norm_gain, w_in, q_gain, k_gain, w_out):
    w = _pad_cols(w_in, LANES).astype(BF16)
    proj = norm_matmul(h2d, norm_gain, w, tm=512, tn=w.shape[1] // 3)
    q, k, vt, iq, ik = attn_prep(proj, q_gain, k_gain, b, t, tt=256)
    iw_off = N_HEADS * HEAD_DIM + 2 * N_KV_HEADS * HEAD_DIM + IDX_HEADS * IDX_DIM + IDX_DIM
    iw = proj[:, iw_off:iw_off + IDX_HEADS].reshape(b, t, IDX_HEADS)
    o = sparse_attention(q, k, vt, iq, ik, iw, qb=128, kc=512)
    return matmul_residual(o.reshape(b * t, -1), w_out.astype(BF16), h2d, tm=512, tn=512)


def _peer_q_body(x_ref, g_ref, w_ref, qt_ref, xnt_ref):
    @pl.when(pl.program_id(1) == 0)
    def _():
        x = x_ref[...]
        ms = jnp.mean(x * x, axis=-1, keepdims=True)
        xnt_ref[...] = (x * lax.rsqrt(ms + RMS_EPS) * g_ref[...]).T.astype(BF16)

    qt_ref[...] = jnp.dot(w_ref[...], xnt_ref[...], preferred_element_type=F32)


def peer_query(h2d, gain, w_qt, *, tm, tn):
    m, k = h2d.shape
    n = w_qt.shape[0]
    return pl.pallas_call(
        _peer_q_body,
        grid=(m // tm, n // tn),
        in_specs=[
            pl.BlockSpec((tm, k), lambda i, j: (i, 0)),
            pl.BlockSpec((1, k), lambda i, j: (0, 0)),
            pl.BlockSpec((tn, k), lambda i, j: (j, 0)),
        ],
        out_specs=[
            pl.BlockSpec((tn, tm), lambda i, j: (j, i)),
            pl.BlockSpec((k, tm), lambda i, j: (0, i)),
        ],
        out_shape=[jax.ShapeDtypeStruct((n, m), F32), jax.ShapeDtypeStruct((k, m), BF16)],
        compiler_params=_cparams("parallel", "arbitrary"),
        name="peer_query",
    )(h2d, gain.reshape(1, k), w_qt)


def _top_values(x, dst_ref, n):
    rank = jnp.full(x.shape, float(n + 1), F32)
    for r in range(n):
        m = x.max(axis=0, keepdims=True)
        dst_ref[r:r + 1, :] = m
        hit = x == m
        rank = jnp.where(hit, float(r + 1), rank)
        x = jnp.where(hit, -jnp.inf, x)
    return rank


def _peer_route_body(qt_ref, keys_ref, n_ref, e0_ref, rank_ref, e1_ref, a_ref, b_ref, cand_ref, f_ref):
    k = PEER_TOPK
    for h in range(PEER_HEADS):
        s, ranks = [], []
        for p, dst in ((0, a_ref), (1, b_ref)):
            r0 = (h * 2 + p) * PEER_NKEYS
            sp = jnp.dot(keys_ref[h, p], qt_ref[r0:r0 + PEER_NKEYS, :],
                         precision=lax.Precision.HIGHEST, preferred_element_type=F32)
            s.append(sp)
            ranks.append(_top_values(sp, dst, k))
        spans, off = [], 0
        for i in range(k):
            spans.append((off, k // (i + 1)))
            off += k // (i + 1)
        cand_ref[off - off % 8:, :] = jnp.full((cand_ref.shape[0] - off + off % 8, cand_ref.shape[1]), -jnp.inf, F32)
        for i, (o, n) in enumerate(spans):
            cand_ref[o:o + n, :] = a_ref[i:i + 1, :] + b_ref[0:n, :]
        _top_values(cand_ref[...], f_ref, k)
        f = f_ref[...]
        z = jnp.exp(f - f[0:1, :]).sum(axis=0, keepdims=True)
        tau = f[k - 1:k, :]
        partners = jnp.zeros(s[0].shape, F32)
        for i, (o, n) in enumerate(spans):
            cnt = jnp.where(cand_ref[o:o + n, :] >= tau, 1.0, 0.0).sum(axis=0, keepdims=True)
            partners = jnp.where(ranks[0] == float(i + 1), cnt, partners)
        n_ref[h] = partners
        e0_ref[h] = jnp.exp(s[0] - a_ref[0:1, :])
        rank_ref[h] = ranks[1].astype(rank_ref.dtype)
        e1_ref[h] = (jnp.exp(s[1] - b_ref[0:1, :]) / z).astype(e1_ref.dtype)


def peer_route(qt, keys, *, tt):
    n, m = qt.shape
    spec = pl.BlockSpec((PEER_HEADS, PEER_NKEYS, tt), lambda i: (0, 0, i))
    shp = jax.ShapeDtypeStruct((PEER_HEADS, PEER_NKEYS, m), F32)
    shp16 = jax.ShapeDtypeStruct((PEER_HEADS, PEER_NKEYS, m), BF16)
    n_cand = sum(PEER_TOPK // (i + 1) for i in range(PEER_TOPK))
    return pl.pallas_call(
        _peer_route_body,
        grid=(m // tt,),
        in_specs=[
            pl.BlockSpec((n, tt), lambda i: (0, i)),
            pl.BlockSpec(keys.shape, lambda i: (0, 0, 0, 0)),
        ],
        out_specs=[spec, spec, spec, spec],
        out_shape=[shp, shp, shp16, shp16],
        scratch_shapes=[
            pltpu.VMEM((PEER_TOPK, tt), F32),
            pltpu.VMEM((PEER_TOPK, tt), F32),
            pltpu.VMEM((-(-n_cand // 8) * 8, tt), F32),
            pltpu.VMEM((PEER_TOPK, tt), F32),
        ],
        compiler_params=_cparams("parallel"),
        name="peer_route",
    )(qt, keys)


def _gelu(x):
    return 0.5 * x * (1.0 + lax.erf(x * (2.0 ** -0.5)))


def _peer_expert_body(xnt_ref, u_ref, vt_ref, n_ref, e0_ref, rank_ref, e1_ref, h_ref, o_ref, acc_ref, coef_ref, *, rows):
    e = pl.program_id(1)

    @pl.when(e == 0)
    def _():
        acc_ref[...] = jnp.zeros(acc_ref.shape, F32)

    act = jnp.dot(u_ref[...], xnt_ref[...], preferred_element_type=F32)
    for ii in range(rows):
        w = None
        for hd in range(PEER_HEADS):
            npart = n_ref[hd, ii:ii + 1, :].astype(BF16)
            e0 = e0_ref[hd, ii:ii + 1, :].astype(BF16)
            t = jnp.where(rank_ref[hd] <= npart, e1_ref[hd], 0.0) * e0
            w = t if w is None else w + t
        sl = slice(ii * PEER_NKEYS, (ii + 1) * PEER_NKEYS)
        coef_ref[sl, :] = w * _gelu(act[sl, :]).astype(BF16)
    acc_ref[...] += jnp.dot(vt_ref[...], coef_ref[...], preferred_element_type=F32)

    @pl.when(e == pl.num_programs(1) - 1)
    def _():
        o_ref[...] = h_ref[...] + acc_ref[...].T


def peer_experts(xnt, u, vt, npart, e0, rank, e1, h2d, *, tt, te):
    d, m = xnt.shape
    n_exp = u.shape[0]
    rows = te // PEER_NKEYS
    row_spec = pl.BlockSpec((PEER_HEADS, rows, tt), lambda i, e: (0, e, i))
    col_spec = pl.BlockSpec((PEER_HEADS, PEER_NKEYS, tt), lambda i, e: (0, 0, i))
    tok_spec = pl.BlockSpec((tt, d), lambda i, e: (i, 0))
    return pl.pallas_call(
        functools.partial(_peer_expert_body, rows=rows),
        grid=(m // tt, n_exp // te),
        in_specs=[
            pl.BlockSpec((d, tt), lambda i, e: (0, i)),
            pl.BlockSpec((te, d), lambda i, e: (e, 0)),
            pl.BlockSpec((d, te), lambda i, e: (0, e)),
            row_spec, row_spec, col_spec, col_spec,
            tok_spec,
        ],
        out_specs=tok_spec,
        out_shape=jax.ShapeDtypeStruct((m, d), F32),
        scratch_shapes=[pltpu.VMEM((d, tt), F32), pltpu.VMEM((te, tt), BF16)],
        compiler_params=_cparams("parallel", "arbitrary"),
        name="peer_experts",
    )(xnt, u, vt, npart, e0, rank, e1, h2d)


def peer_mixer(h2d, gain, w_q, keys, u_tab, v_tab):
    qt, xnt = peer_query(h2d, gain, w_q.T.astype(BF16), tm=512, tn=512)
    npart, e0, rank, e1 = peer_route(qt, keys, tt=256)
    return peer_experts(xnt, u_tab.astype(BF16), v_tab.T.astype(BF16), npart, e0, rank, e1, h2d, tt=512, te=1024)


def _ple_body(x_ref, g_ref, wg_ref, p_ref, wi_ref, r_ref, o_ref, xn_ref):
    @pl.when(pl.program_id(1) == 0)
    def _():
        x = x_ref[...]
        ms = jnp.mean(x * x, axis=-1, keepdims=True)
        xn_ref[...] = (x * lax.rsqrt(ms + RMS_EPS) * g_ref[...]).astype(BF16)

    gate = jax.nn.sigmoid(jnp.dot(xn_ref[...], wg_ref[...], preferred_element_type=F32))
    emb = jnp.dot(p_ref[...], wi_ref[...], preferred_element_type=F32)
    o_ref[...] = r_ref[...] + gate * emb


def ple_mixer(h2d, gain, w_gate, p2d, w_in, *, tm, tn):
    m, d = h2d.shape
    pd = p2d.shape[1]
    return pl.pallas_call(
        _ple_body,
        grid=(m // tm, d // tn),
        in_specs=[
            pl.BlockSpec((tm, d), lambda i, j: (i, 0)),
            pl.BlockSpec((1, d), lambda i, j: (0, 0)),
            pl.BlockSpec((d, tn), lambda i, j: (0, j)),
            pl.BlockSpec((tm, pd), lambda i, j: (i, 0)),
            pl.BlockSpec((pd, tn), lambda i, j: (0, j)),
            pl.BlockSpec((tm, tn), lambda i, j: (i, j)),
        ],
        out_specs=pl.BlockSpec((tm, tn), lambda i, j: (i, j)),
        out_shape=jax.ShapeDtypeStruct((m, d), F32),
        scratch_shapes=[pltpu.VMEM((tm, d), BF16)],
        compiler_params=_cparams("parallel", "arbitrary"),
        name="ple_mixer",
    )(h2d, gain.reshape(1, d), w_gate.astype(BF16), p2d.astype(BF16), w_in.astype(BF16), h2d)


def _dn_conv_body(x_ref, halo_ref, w_ref, *rest, mode, tt, hb):
    if mode == "k":
        o_ref, ot_ref, ext_ref = rest
    else:
        o_ref, ext_ref = rest
    width, d = DN_CONV_WIDTH, DN_HEAD_DIM
    first = pl.program_id(1) == 0
    ext_ref[0:8, :] = jnp.where(first, 0.0, halo_ref[...])
    ext_ref[8:, :] = x_ref[...]
    for h in range(hb):
        lanes = slice(h * d, (h + 1) * d)
        y = None
        for j in range(width):
            term = w_ref[j:j + 1, lanes] * ext_ref[pl.ds(8 - (width - 1) + j, tt), lanes]
            y = term if y is None else y + term
        y = y * jax.nn.sigmoid(y)
        if mode in ("q", "k"):
            y = y * lax.rsqrt(jnp.sum(y * y, axis=-1, keepdims=True) + RMS_EPS)
        if mode == "q":
            y = y * (d ** -0.5)
        o_ref[0, h] = y.astype(o_ref.dtype)
        if mode == "k":
            ot_ref[0, h] = y.T.astype(ot_ref.dtype)


def dn_conv_silu(proj, conv_w, b, t, *, mode, col0, heads, tt, hb):
    nt = t // tt
    d = DN_HEAD_DIM
    cb0 = col0 // (hb * d)
    out_spec = pl.BlockSpec((1, hb, tt, d), lambda bi, ti, hi: (bi, hi, ti, 0))
    out_shape = jax.ShapeDtypeStruct((b, heads, t, d), F32)
    out_specs, out_shapes = [out_spec], [out_shape]
    if mode == "k":
        out_specs.append(pl.BlockSpec((1, hb, d, tt), lambda bi, ti, hi: (bi, hi, 0, ti)))
        out_shapes.append(jax.ShapeDtypeStruct((b, heads, d, t), F32))
    res = pl.pallas_call(
        functools.partial(_dn_conv_body, mode=mode, tt=tt, hb=hb),
        grid=(b, nt, heads // hb),
        in_specs=[
            pl.BlockSpec((tt, hb * d), lambda bi, ti, hi: (bi * nt + ti, cb0 + hi)),
            pl.BlockSpec((8, hb * d), lambda bi, ti, hi: (jnp.maximum((bi * nt + ti) * (tt // 8) - 1, 0), cb0 + hi)),
            pl.BlockSpec((DN_CONV_WIDTH, hb * d), lambda bi, ti, hi: (0, cb0 + hi)),
        ],
        out_specs=out_specs,
        out_shape=out_shapes,
        scratch_shapes=[pltpu.VMEM((tt + 8, hb * d), F32)],
        compiler_params=_cparams("parallel", "parallel", "parallel"),
        name="dn_conv_" + mode,
    )(proj, proj, conv_w)
    return res if mode == "k" else res[0]


def _dn_gate_body(ba_ref, alog_ref, dtb_ref, beta_ref, gcum_ref, *, tm):
    nh = DN_V_HEADS
    beta_ref[...] = jax.nn.sigmoid(ba_ref[0:nh, :])
    a = ba_ref[nh:2 * nh, :]
    g = -jnp.exp(alog_ref[...]) * jax.nn.softplus(a + dtb_ref[...])
    lane = lax.broadcasted_iota(I32, (nh, LANES), 1) % DN_CHUNK
    for j in range(tm // LANES):
        x = g[:, j * LANES:(j + 1) * LANES]
        s = 1
        while s < DN_CHUNK:
            x = x + jnp.where(lane >= s, pltpu.roll(x, s, 1), 0.0)
            s *= 2
        gcum_ref[:, j * LANES:(j + 1) * LANES] = x


def dn_gates(ba_t, a_log, dt_bias, *, tm):
    m = ba_t.shape[1]
    nh = DN_V_HEADS
    spec = pl.BlockSpec((nh, tm), lambda i: (0, i))
    col = pl.BlockSpec((nh, 1), lambda i: (0, 0))
    return pl.pallas_call(
        functools.partial(_dn_gate_body, tm=tm),
        grid=(m // tm,),
        in_specs=[pl.BlockSpec((ba_t.shape[0], tm), lambda i: (0, i)), col, col],
        out_specs=[spec, spec],
        out_shape=[jax.ShapeDtypeStruct((nh, m), F32)] * 2,
        compiler_params=_cparams("parallel"),
        name="dn_gates",
    )(ba_t, a_log.reshape(nh, 1), dt_bias.reshape(nh, 1))


def _unit_lower_inverse(a, eye):
    c = a.shape[0]
    inv = eye - a
    x = a
    span = 2
    while span < c:
        x = jnp.dot(x.astype(BF16), x.astype(BF16), preferred_element_type=F32)
        inv = inv + jnp.dot(inv.astype(BF16), x.astype(BF16), preferred_element_type=F32)
        span *= 2
    return inv


def _dn_delta_body(q_ref, k_ref, kt_ref, v_ref, z_ref, gc_ref, gr_ref, bc_ref, gain_ref, o_ref, state_ref, *, gsz, nchunk):
    c, d = DN_CHUNK, DN_HEAD_DIM
    rep = DN_V_HEADS // DN_QK_HEADS

    @pl.when(pl.program_id(2) == 0)
    def _():
        state_ref[...] = jnp.zeros(state_ref.shape, F32)

    row = lax.broadcasted_iota(I32, (c, c), 0)
    colm = lax.broadcasted_iota(I32, (c, c), 1)
    tril, strict = row >= colm, row > colm
    eye = jnp.where(row == colm, 1.0, 0.0)
    gain = gain_ref[...]
    for cc in range(nchunk):
        rows = slice(cc * c, (cc + 1) * c)
        for jq in range(gsz // rep):
            q = q_ref[0, jq, rows, :]
            k = k_ref[0, jq, rows, :]
            kt = kt_ref[0, jq, :, rows]
            kk = lax.dot_general(k.astype(BF16), k.astype(BF16), _NT, preferred_element_type=F32)
            qk = lax.dot_general(q.astype(BF16), k.astype(BF16), _NT, preferred_element_type=F32)
            for jv in range(jq * rep, (jq + 1) * rep):
                gc = gc_ref[0, 0, rows, jv:jv + 1]
                gr = gr_ref[0, 0, jv:jv + 1, rows]
                beta = bc_ref[0, 0, rows, jv:jv + 1]
                decay = jnp.exp(jnp.where(tril, gc - gr, NEG_BIG))
                a = jnp.where(strict, kk * decay, 0.0) * beta
                inv = _unit_lower_inverse(a, eye)
                v = v_ref[0, jv, rows, :]
                eg = jnp.exp(gc)
                rhs = jnp.concatenate([v * beta, k * (beta * eg)], axis=1)
                sol = jnp.dot(inv.astype(BF16), rhs.astype(BF16), preferred_element_type=F32)
                u, w = sol[:, :d], sol[:, d:]
                s_old = state_ref[jv]
                r1 = jnp.dot(jnp.concatenate([w, q * eg], axis=0).astype(BF16), s_old.astype(BF16),
                             preferred_element_type=F32)
                v_new = u - r1[:c]
                g_last = gc[c - 1:c, :]
                kd_t = kt * jnp.exp(g_last - gr)
                r2 = jnp.dot(jnp.concatenate([qk * decay, kd_t], axis=0).astype(BF16), v_new.astype(BF16),
                             preferred_element_type=F32)
                o = r1[c:] + r2[:c]
                state_ref[jv] = s_old * jnp.exp(g_last) + r2[c:]
                ms = jnp.mean(o * o, axis=-1, keepdims=True)
                z = z_ref[rows, jv * d:(jv + 1) * d]
                o_ref[rows, jv * d:(jv + 1) * d] = (
                    o * lax.rsqrt(ms + RMS_EPS) * gain * (z * jax.nn.sigmoid(z))).astype(o_ref.dtype)


def dn_delta(q, k, kt, v, proj, z_col0, gc_col, gc_row, beta_col, norm_gain, *, gsz, tb):
    b, hv, t, d = v.shape
    rep = DN_V_HEADS // DN_QK_HEADS
    hg = hv // gsz
    nb = t // tb
    zb0 = z_col0 // (gsz * d)
    return pl.pallas_call(
        functools.partial(_dn_delta_body, gsz=gsz, nchunk=tb // DN_CHUNK),
        grid=(b, hg, nb),
        in_specs=[
            pl.BlockSpec((1, gsz // rep, tb, d), lambda bi, gi, ni: (bi, gi, ni, 0)),
            pl.BlockSpec((1, gsz // rep, tb, d), lambda bi, gi, ni: (bi, gi, ni, 0)),
            pl.BlockSpec((1, gsz // rep, d, tb), lambda bi, gi, ni: (bi, gi, 0, ni)),
            pl.BlockSpec((1, gsz, tb, d), lambda bi, gi, ni: (bi, gi, ni, 0)),
            pl.BlockSpec((tb, gsz * d), lambda bi, gi, ni: (bi * nb + ni, zb0 + gi)),
            pl.BlockSpec((1, 1, tb, gsz), lambda bi, gi, ni: (bi, gi, ni, 0)),
            pl.BlockSpec((1, 1, gsz, tb), lambda bi, gi, ni: (bi, gi, 0, ni)),
            pl.BlockSpec((1, 1, tb, gsz), lambda bi, gi, ni: (bi, gi, ni, 0)),
            pl.BlockSpec((1, d), lambda bi, gi, ni: (0, 0)),
        ],
        out_specs=pl.BlockSpec((tb, gsz * d), lambda bi, gi, ni: (bi * nb + ni, gi)),
        out_shape=jax.ShapeDtypeStruct((b * t, hv * d), BF16),
        scratch_shapes=[pltpu.VMEM((gsz, d, d), F32)],
        compiler_params=_cparams("parallel", "parallel", "arbitrary"),
        name="dn_delta",
    )(q, k, kt, v, proj, gc_col, gc_row, beta_col, norm_gain.reshape(1, d))


def _dn_local_body(q_ref, k_ref, kt_ref, v_ref, gc_ref, gr_ref, bc_ref, u_ref, l1_ref, l2_ref, eg_ref, *, gsz, nchunk):
    c, d = DN_CHUNK, DN_HEAD_DIM
    rep = DN_V_HEADS // DN_QK_HEADS
    row = lax.broadcasted_iota(I32, (c, c), 0)
    colm = lax.broadcasted_iota(I32, (c, c), 1)
    tril, strict = row >= colm, row > colm
    eye = jnp.where(row == colm, 1.0, 0.0)
    bodies = [(cc, jv) for cc in range(nchunk) for jv in range(gsz)]

    kk, qk = {}, {}
    for cc in range(nchunk):
        rows = slice(cc * c, (cc + 1) * c)
        for jq in range(gsz // rep):
            kb = k_ref[0, jq, rows, :].astype(BF16)
            qb = q_ref[0, jq, rows, :].astype(BF16)
            kk[cc, jq] = lax.dot_general(kb, kb, _NT, preferred_element_type=F32)
            qk[cc, jq] = lax.dot_general(qb, kb, _NT, preferred_element_type=F32)

    x, inv, rhs = {}, {}, {}
    for cc, jv in bodies:
        rows = slice(cc * c, (cc + 1) * c)
        jq = jv // rep
        gc = gc_ref[0, 0, rows, jv:jv + 1]
        gr = gr_ref[0, 0, jv:jv + 1, rows]
        beta = bc_ref[0, 0, rows, jv:jv + 1]
        decay = jnp.exp(jnp.where(tril, gc - gr, NEG_BIG))
        a = jnp.where(strict, kk[cc, jq] * decay, 0.0) * beta
        x[cc, jv] = a
        inv[cc, jv] = eye - a
        eg = jnp.exp(gc)
        k = k_ref[0, jq, rows, :]
        rhs[cc, jv] = jnp.concatenate([v_ref[0, jv, rows, :] * beta, k * (beta * eg)], axis=1).astype(BF16)
        g_last = gc[c - 1:c, :]
        kd_t = kt_ref[0, jq, :, rows] * jnp.exp(g_last - gr)
        l2_ref[0, jv, cc] = jnp.concatenate([qk[cc, jq] * decay, kd_t], axis=0).astype(l2_ref.dtype)
        l1_ref[0, jv, cc, c:2 * c, :] = (q_ref[0, jq, rows, :] * eg).astype(l1_ref.dtype)
        eg_ref[0, jv, cc] = jnp.broadcast_to(jnp.exp(g_last), (1, d))

    span = 2
    while span < c:
        for key in bodies:
            xb = x[key].astype(BF16)
            x[key] = jnp.dot(xb, xb, preferred_element_type=F32)
        for key in bodies:
            inv[key] = inv[key] + jnp.dot(inv[key].astype(BF16), x[key].astype(BF16), preferred_element_type=F32)
        span *= 2

    for cc, jv in bodies:
        rows = slice(cc * c, (cc + 1) * c)
        sol = jnp.dot(inv[cc, jv].astype(BF16), rhs[cc, jv], preferred_element_type=F32)
        u_ref[0, jv, rows, :] = sol[:, :d]
        l1_ref[0, jv, cc, 0:c, :] = sol[:, d:].astype(l1_ref.dtype)


def dn_local(q, k, kt, v, gc_col, gc_row, beta_col, *, gsz, tb):
    b, hv, t, d = v.shape
    c = DN_CHUNK
    rep = DN_V_HEADS // DN_QK_HEADS
    nb = t // tb
    nchunk = tb // c
    qk_spec = pl.BlockSpec((1, gsz // rep, tb, d), lambda bi, gi, ni: (bi, gi, ni, 0))
    col_spec = pl.BlockSpec((1, 1, tb, gsz), lambda bi, gi, ni: (bi, gi, ni, 0))

    def chunked(rows, lanes):
        return pl.BlockSpec((1, gsz, nchunk, rows, lanes), lambda bi, gi, ni: (bi, gi, ni, 0, 0))

    return pl.pallas_call(
        functools.partial(_dn_local_body, gsz=gsz, nchunk=nchunk),
        grid=(b, hv // gsz, nb),
        in_specs=[
            qk_spec, qk_spec,
            pl.BlockSpec((1, gsz // rep, d, tb), lambda bi, gi, ni: (bi, gi, 0, ni)),
            pl.BlockSpec((1, gsz, tb, d), lambda bi, gi, ni: (bi, gi, ni, 0)),
            col_spec,
            pl.BlockSpec((1, 1, gsz, tb), lambda bi, gi, ni: (bi, gi, 0, ni)),
            col_spec,
        ],
        out_specs=[
            pl.BlockSpec((1, gsz, tb, d), lambda bi, gi, ni: (bi, gi, ni, 0)),
            chunked(2 * c, d), chunked(c + d, c), chunked(1, d),
        ],
        out_shape=[
            jax.ShapeDtypeStruct((b, hv, t, d), F32),
            jax.ShapeDtypeStruct((b, hv, t // c, 2 * c, d), BF16),
            jax.ShapeDtypeStruct((b, hv, t // c, c + d, c), BF16),
            jax.ShapeDtypeStruct((b, hv, t // c, 1, d), F32),
        ],
        compiler_params=_cparams("parallel", "parallel", "parallel"),
        name="dn_local",
    )(q, k, kt, v, gc_col, gc_row, beta_col)


def _dn_scan_body(u_ref, l1_ref, l2_ref, eg_ref, z_ref, gain_ref, o_ref, state_ref, *, gsz, nchunk):
    c, d = DN_CHUNK, DN_HEAD_DIM

    @pl.when(pl.program_id(2) == 0)
    def _():
        state_ref[...] = jnp.zeros(state_ref.shape, F32)

    gain = gain_ref[...]
    heads = range(gsz)
    for cc in range(nchunk):
        rows = slice(cc * c, (cc + 1) * c)
        r1 = [jnp.dot(l1_ref[0, jv, cc], state_ref[jv].astype(BF16), preferred_element_type=F32) for jv in heads]
        v_new = [(u_ref[0, jv, rows, :] - r1[jv][:c]).astype(BF16) for jv in heads]
        r2 = [jnp.dot(l2_ref[0, jv, cc], v_new[jv], preferred_element_type=F32) for jv in heads]
        for jv in heads:
            state_ref[jv] = state_ref[jv] * eg_ref[0, jv, cc] + r2[jv][c:]
        for jv in heads:
            o = r1[jv][c:] + r2[jv][:c]
            ms = jnp.mean(o * o, axis=-1, keepdims=True)
            z = z_ref[rows, jv * d:(jv + 1) * d]
            o_ref[rows, jv * d:(jv + 1) * d] = (
                o * lax.rsqrt(ms + RMS_EPS) * gain * (z * jax.nn.sigmoid(z))).astype(o_ref.dtype)


def dn_scan(u, l1, l2, eg, proj, z_col0, norm_gain, *, gsz, tb):
    b, hv, t, d = u.shape
    c = DN_CHUNK
    nb = t // tb
    nchunk = tb // c
    zb0 = z_col0 // (gsz * d)

    def chunked(rows, lanes):
        return pl.BlockSpec((1, gsz, nchunk, rows, lanes), lambda bi, gi, ni: (bi, gi, ni, 0, 0))

    return pl.pallas_call(
        functools.partial(_dn_scan_body, gsz=gsz, nchunk=nchunk),
        grid=(b, hv // gsz, nb),
        in_specs=[
            pl.BlockSpec((1, gsz, tb, d), lambda bi, gi, ni: (bi, gi, ni, 0)),
            chunked(2 * c, d), chunked(c + d, c), chunked(1, d),
            pl.BlockSpec((tb, gsz * d), lambda bi, gi, ni: (bi * nb + ni, zb0 + gi)),
            pl.BlockSpec((1, d), lambda bi, gi, ni: (0, 0)),
        ],
        out_specs=pl.BlockSpec((tb, gsz * d), lambda bi, gi, ni: (bi * nb + ni, gi)),
        out_shape=jax.ShapeDtypeStruct((b * t, hv * d), BF16),
        scratch_shapes=[pltpu.VMEM((gsz, d, d), F32)],
        compiler_params=_cparams("parallel", "parallel", "arbitrary"),
        name="dn_scan",
    )(u, l1, l2, eg, proj, norm_gain.reshape(1, d))


def gdn_mixer(h2d, b, t, norm_gain, w_in, conv_w, a_log, dt_bias, out_gain, w_out):
    key_dim = DN_QK_HEADS * DN_HEAD_DIM
    val_dim = DN_V_HEADS * DN_HEAD_DIM
    main = 2 * key_dim + 2 * val_dim
    proj = norm_matmul(h2d, norm_gain, w_in[:, :main].astype(BF16), tm=512, tn=1024)
    w_ba_t = _pad_cols(w_in[:, main:], LANES).T.astype(BF16)
    ba_t, _ = peer_query(h2d, norm_gain, w_ba_t, tm=512, tn=LANES)
    beta_t, gcum_t = dn_gates(ba_t, a_log, dt_bias, tm=512)
    q = dn_conv_silu(proj, conv_w, b, t, mode="q", col0=0, heads=DN_QK_HEADS, tt=512, hb=8)
    k, kt = dn_conv_silu(proj, conv_w, b, t, mode="k", col0=key_dim, heads=DN_QK_HEADS, tt=512, hb=8)
    v = dn_conv_silu(proj, conv_w, b, t, mode="v", col0=2 * key_dim, heads=DN_V_HEADS, tt=512, hb=8)
    gsz = 8
    hg = DN_V_HEADS // gsz
    rows = lambda x: x.reshape(hg, gsz, b, t).transpose(2, 0, 1, 3)
    cols = lambda x: rows(x).transpose(0, 1, 3, 2)
    u, l1, l2, eg = dn_local(q, k, kt, v, cols(gcum_t), rows(gcum_t), cols(beta_t), gsz=gsz, tb=2 * DN_CHUNK)
    o = dn_scan(u, l1, l2, eg, proj, 2 * key_dim + val_dim, out_gain, gsz=gsz, tb=4 * DN_CHUNK)
    return matmul_residual(o, w_out.astype(BF16), h2d, tm=512, tn=512)


def kernel(x, p, norm_mix, norm_ffn, norm_ple, attn_w_in, attn_q_norm, attn_k_norm, attn_w_out, dn_w_in, dn_conv,
           dn_a_log, dn_dt_bias, dn_norm, dn_w_out, peer_w_q, peer_keys, peer_u, peer_v, ple_w_in, ple_w_gate):
    b, t, d = x.shape
    n_mixers = 2
    h = x.reshape(b * t, d)
    for i in range(p.shape[0]):
        j = i // n_mixers
        if i % n_mixers == 0:
            h = dsa_mixer(h, b, t, norm_mix[i], attn_w_in[j], attn_q_norm[j], attn_k_norm[j], attn_w_out[j])
        else:
            h = gdn_mixer(h, b, t, norm_mix[i], dn_w_in[j], dn_conv[j], dn_a_log[j], dn_dt_bias[j], dn_norm[j],
                          dn_w_out[j])
        h = peer_mixer(h, norm_ffn[i], peer_w_q[i], peer_keys[i], peer_u[i], peer_v[i])
        h = ple_mixer(h, norm_ple[i], ple_w_gate[i], p[i].reshape(b * t, -1), ple_w_in[i], tm=512, tn=512)
    return h.reshape(b, t, d)
```

```python
import functools

import jax
import jax.numpy as jnp
import numpy as np
from jax import lax
from jax.experimental import pallas as pl
from jax.experimental.pallas import tpu as pltpu

F32 = jnp.float32
BF16 = jnp.bfloat16
I32 = jnp.int32
I16 = jnp.int16
HALF16 = 2 ** 15

RMS_EPS = 1e-6
ROPE_THETA = 500000.0
ROT_FRACTION = 4

N_HEADS = 16
N_KV_HEADS = 4
HEAD_DIM = 128
IDX_HEADS = 16
IDX_DIM = 64
INDEX_TOPK = 256
V_ROWS = HEAD_DIM + 16

DN_QK_HEADS = 16
DN_V_HEADS = 32
DN_HEAD_DIM = 128
DN_CONV_WIDTH = 4
DN_CHUNK = 64

PEER_HEADS = 8
PEER_NKEYS = 128
PEER_TOPK = 16

LANES = 128
VMEM_LIMIT = 56 * 1024 * 1024

LOG2E = 1.4426950408889634
INT_MIN = -(2 ** 31)
NEG_BIG = -1e30

_NT = (((1,), (1,)), ((), ()))


def _cparams(*sem):
    return pltpu.CompilerParams(dimension_semantics=sem, vmem_limit_bytes=VMEM_LIMIT)


def _norm_mm_body(x_ref, g_ref, w_ref, o_ref, xn_ref):
    @pl.when(pl.program_id(1) == 0)
    def _():
        x = x_ref[...]
        ms = jnp.mean(x * x, axis=-1, keepdims=True)
        xn_ref[...] = (x * lax.rsqrt(ms + RMS_EPS) * g_ref[...]).astype(BF16)

    o_ref[...] = jnp.dot(xn_ref[...], w_ref[...], preferred_element_type=F32).astype(o_ref.dtype)


def norm_matmul(x, gain, w, *, tm, tn, out_dtype=F32):
    m, k = x.shape
    n = w.shape[1]
    return pl.pallas_call(
        _norm_mm_body,
        grid=(m // tm, n // tn),
        in_specs=[
            pl.BlockSpec((tm, k), lambda i, j: (i, 0)),
            pl.BlockSpec((1, k), lambda i, j: (0, 0)),
            pl.BlockSpec((k, tn), lambda i, j: (0, j)),
        ],
        out_specs=pl.BlockSpec((tm, tn), lambda i, j: (i, j)),
        out_shape=jax.ShapeDtypeStruct((m, n), out_dtype),
        scratch_shapes=[pltpu.VMEM((tm, k), BF16)],
        compiler_params=_cparams("parallel", "arbitrary"),
        name="norm_matmul",
    )(x, gain.reshape(1, k), w)


def _mm_res_body(x_ref, w_ref, r_ref, o_ref):
    o_ref[...] = r_ref[...] + jnp.dot(x_ref[...], w_ref[...], preferred_element_type=F32)


def matmul_residual(x, w, res, *, tm, tn):
    m, k = x.shape
    n = w.shape[1]
    return pl.pallas_call(
        _mm_res_body,
        grid=(m // tm, n // tn),
        in_specs=[
            pl.BlockSpec((tm, k), lambda i, j: (i, 0)),
            pl.BlockSpec((k, tn), lambda i, j: (0, j)),
            pl.BlockSpec((tm, tn), lambda i, j: (i, j)),
        ],
        out_specs=pl.BlockSpec((tm, tn), lambda i, j: (i, j)),
        out_shape=jax.ShapeDtypeStruct((m, n), F32),
        compiler_params=_cparams("parallel", "arbitrary"),
        name="matmul_residual",
    )(x, w, res)


def _rope_tables(t, head_dim):
    rot = head_dim // ROT_FRACTION
    half = rot // 2
    inv_freq = ROPE_THETA ** (-jnp.arange(half, dtype=F32) * (2.0 / rot))
    ang = jnp.arange(t, dtype=jnp.int32).astype(F32)[:, None] * inv_freq[None, :]
    cos, sin = jnp.cos(ang), jnp.sin(ang)
    rest = head_dim - rot
    c = jnp.concatenate([cos, cos, jnp.ones((t, rest), F32)], axis=-1)
    s_lo = jnp.concatenate([-sin, jnp.zeros((t, half + rest), F32)], axis=-1)
    s_hi = jnp.concatenate([jnp.zeros((t, half), F32), sin, jnp.zeros((t, rest), F32)], axis=-1)
    reps = LANES // head_dim
    return tuple(jnp.tile(a, (1, reps)) for a in (c, s_lo, s_hi)), half


def _rope_tile(x, c, s_lo, s_hi, half):
    return x * c + pltpu.roll(x, LANES - half, 1) * s_lo + pltpu.roll(x, half, 1) * s_hi


def _attn_prep_body(p_ref, qg_ref, kg_ref, c128_ref, sl128_ref, sh128_ref, c64_ref, sl64_ref, sh64_ref,
                    q_ref, k_ref, vt_ref, iq_ref, ik_ref, *, half128, half64):
    c128, sl128, sh128 = c128_ref[...], sl128_ref[...], sh128_ref[...]
    c64, sl64, sh64 = c64_ref[...], sl64_ref[...], sh64_ref[...]

    def normed(x, g):
        ms = jnp.mean(x * x, axis=-1, keepdims=True)
        return x * lax.rsqrt(ms + RMS_EPS) * g

    off = 0
    for h in range(N_HEADS):
        x = normed(p_ref[:, off:off + HEAD_DIM], qg_ref[...])
        q_ref[0, h] = (_rope_tile(x, c128, sl128, sh128, half128) * (HEAD_DIM ** -0.5 * LOG2E)).astype(BF16)
        off += HEAD_DIM
    for h in range(N_KV_HEADS):
        x = normed(p_ref[:, off:off + HEAD_DIM], kg_ref[...])
        k_ref[0, h] = _rope_tile(x, c128, sl128, sh128, half128).astype(BF16)
        off += HEAD_DIM
    ones_rows = jnp.ones((V_ROWS - HEAD_DIM, p_ref.shape[0]), BF16)
    for h in range(N_KV_HEADS):
        vt_ref[0, h, 0:HEAD_DIM, :] = p_ref[:, off:off + HEAD_DIM].T.astype(BF16)
        vt_ref[0, h, HEAD_DIM:V_ROWS, :] = ones_rows
        off += HEAD_DIM
    for j in range(IDX_HEADS * IDX_DIM // LANES):
        x = _rope_tile(p_ref[:, off:off + LANES], c64, sl64, sh64, half64) * (IDX_DIM ** -0.5)
        for u in range(LANES // IDX_DIM):
            iq_ref[0, j * (LANES // IDX_DIM) + u] = x[:, u * IDX_DIM:(u + 1) * IDX_DIM].astype(BF16)
        off += LANES
    x = _rope_tile(p_ref[:, off:off + LANES], c64, sl64, sh64, half64)
    ik_ref[0] = x[:, :IDX_DIM].astype(BF16)


def attn_prep(proj, q_gain, k_gain, b, t, *, tt):
    (c128, sl128, sh128), half128 = _rope_tables(t, HEAD_DIM)
    (c64, sl64, sh64), half64 = _rope_tables(t, IDX_DIM)
    nt = t // tt
    width = proj.shape[1]
    tab = pl.BlockSpec((tt, LANES), lambda bi, ti: (ti, 0))
    gain = pl.BlockSpec((1, HEAD_DIM), lambda bi, ti: (0, 0))
    return pl.pallas_call(
        functools.partial(_attn_prep_body, half128=half128, half64=half64),
        grid=(b, nt),
        in_specs=[pl.BlockSpec((tt, width), lambda bi, ti: (bi * nt + ti, 0)), gain, gain,
                  tab, tab, tab, tab, tab, tab],
        out_specs=[
            pl.BlockSpec((1, N_HEADS, tt, HEAD_DIM), lambda bi, ti: (bi, 0, ti, 0)),
            pl.BlockSpec((1, N_KV_HEADS, tt, HEAD_DIM), lambda bi, ti: (bi, 0, ti, 0)),
            pl.BlockSpec((1, N_KV_HEADS, V_ROWS, tt), lambda bi, ti: (bi, 0, 0, ti)),
            pl.BlockSpec((1, IDX_HEADS, tt, IDX_DIM), lambda bi, ti: (bi, 0, ti, 0)),
            pl.BlockSpec((1, tt, IDX_DIM), lambda bi, ti: (bi, ti, 0)),
        ],
        out_shape=[
            jax.ShapeDtypeStruct((b, N_HEADS, t, HEAD_DIM), BF16),
            jax.ShapeDtypeStruct((b, N_KV_HEADS, t, HEAD_DIM), BF16),
            jax.ShapeDtypeStruct((b, N_KV_HEADS, V_ROWS, t), BF16),
            jax.ShapeDtypeStruct((b, IDX_HEADS, t, IDX_DIM), BF16),
            jax.ShapeDtypeStruct((b, t, IDX_DIM), BF16),
        ],
        compiler_params=_cparams("parallel", "parallel"),
        name="attn_prep",
    )(proj, q_gain.reshape(1, HEAD_DIM), k_gain.reshape(1, HEAD_DIM), c128, sl128, sh128, c64, sl64, sh64)


def _attn_body(iq_ref, iw_ref, ik_ref, q_ref, k_ref, vt_ref, o_ref,
               keys_ref, hi_ref, lo_ref, m_ref, acc_ref, *, qb, kc, topk):
    qi = pl.program_id(1)
    group = N_HEADS // N_KV_HEADS
    nck = ((qi + 1) * qb + kc - 1) // kc
    iq = iq_ref[0].reshape(IDX_HEADS * qb, IDX_DIM)
    iw = iw_ref[0, 0] * (IDX_HEADS ** -0.5)
    qpos = qi * qb + lax.broadcasted_iota(I32, (kc, qb), 1)
    krow = lax.broadcasted_iota(I32, (kc, qb), 0)

    def score_chunk(c, carry):
        off = pl.multiple_of(c * kc, kc)
        ikc = ik_ref[0, pl.ds(off, kc), :]
        s = lax.dot_general(ikc, iq, _NT, preferred_element_type=F32)
        s = jnp.maximum(s, 0.0) * iw
        sc = s[:, 0:qb]
        for h in range(1, IDX_HEADS):
            sc = sc + s[:, h * qb:(h + 1) * qb]
        bits = pltpu.bitcast(sc, I32)
        key = bits ^ ((bits >> 31) & 0x7FFFFFFF)
        key = jnp.where(krow + off <= qpos, key, INT_MIN)
        keys_ref[c] = key
        hi_ref[c] = (key >> 16).astype(I16)
        lo_ref[c] = ((key & 0xFFFF) - HALF16).astype(I16)
        return carry

    lax.fori_loop(0, nck, score_chunk, 0)

    one16, zero16 = jnp.ones((), I16), jnp.zeros((), I16)

    def count_ge(ref, cand):
        cand16 = jnp.broadcast_to(cand.astype(I16), (16, qb))

        def body(c, acc):
            parts = [jnp.where(ref[c, r * 16:(r + 1) * 16, :] >= cand16, one16, zero16) for r in range(kc // 16)]
            while len(parts) > 1:
                parts = [parts[i] + parts[i + 1] for i in range(0, len(parts), 2)]
            return acc + parts[0].astype(I32)

        acc = lax.fori_loop(0, nck, body, jnp.zeros((16, qb), I32))
        return acc.sum(axis=0, keepdims=True)

    def largest_with_count(ref, need):
        zero = jnp.zeros((1, qb), I32)
        ans = jnp.where(count_ge(ref, zero) >= need, zero, -HALF16)

        def bit_step(i, ans):
            cand = ans | lax.shift_left(jnp.int32(1), 14 - i)
            return jnp.where(count_ge(ref, cand) >= need, cand, ans)

        return lax.fori_loop(0, 15, bit_step, ans)

    ans_hi = largest_with_count(hi_ref, topk)
    above = jnp.where(ans_hi == HALF16 - 1, 0, count_ge(hi_ref, jnp.minimum(ans_hi + 1, HALF16 - 1)))
    hi16 = jnp.broadcast_to(ans_hi.astype(I16), (kc, qb))

    def mask_low(c, carry):
        lo_ref[c] = jnp.where(hi_ref[c] == hi16, lo_ref[c], jnp.full((), -HALF16, I16))
        return carry

    lax.fori_loop(0, nck, mask_low, 0)
    ans_lo = largest_with_count(lo_ref, topk - above)
    tau = lax.shift_left(ans_hi, 16) | (ans_lo + HALF16)
    tau = jnp.maximum(tau, INT_MIN + 1)

    m_ref[...] = jnp.full(m_ref.shape, NEG_BIG, F32)
    acc_ref[...] = jnp.zeros(acc_ref.shape, F32)

    def attn_chunk(c, carry):
        off = pl.multiple_of(c * kc, kc)
        bias = jnp.where(keys_ref[c] >= tau, 0.0, NEG_BIG)
        bias = jnp.concatenate([bias] * group, axis=1)
        heads = range(N_KV_HEADS)
        s = [lax.dot_general(k_ref[0, n, pl.ds(off, kc), :],
                             q_ref[0, n * group:(n + 1) * group].reshape(group * qb, HEAD_DIM),
                             _NT, preferred_element_type=F32) + bias for n in heads]
        m_new = [jnp.maximum(m_ref[n], s[n].max(axis=0, keepdims=True)) for n in heads]
        p = [jnp.exp2(s[n] - m_new[n]) for n in heads]
        pv = [jnp.dot(vt_ref[0, n, :, pl.ds(off, kc)], p[n].astype(BF16), preferred_element_type=F32)
              for n in heads]
        for n in heads:
            acc_ref[n] = jnp.exp2(m_ref[n] - m_new[n]) * acc_ref[n] + pv[n]
            m_ref[n] = m_new[n]
        return carry

    lax.fori_loop(0, nck, attn_chunk, 0)

    for n in range(N_KV_HEADS):
        on = acc_ref[n, 0:HEAD_DIM, :] / acc_ref[n, HEAD_DIM:HEAD_DIM + 1, :]
        for g in range(group):
            h = n * group + g
            o_ref[0, :, h * HEAD_DIM:(h + 1) * HEAD_DIM] = on[:, g * qb:(g + 1) * qb].T.astype(o_ref.dtype)


def sparse_attention(q, k, vt, iq, ik, iw, *, qb, kc):
    b, _, t, _ = q.shape
    nq = t // qb
    topk = min(INDEX_TOPK, t // 4)
    group = N_HEADS // N_KV_HEADS
    iw_rows = iw.reshape(b, nq, qb, IDX_HEADS).transpose(0, 1, 3, 2).reshape(b, nq, 1, IDX_HEADS * qb)
    return pl.pallas_call(
        functools.partial(_attn_body, qb=qb, kc=kc, topk=topk),
        grid=(b, nq),
        in_specs=[
            pl.BlockSpec((1, IDX_HEADS, qb, IDX_DIM), lambda bi, qi: (bi, 0, qi, 0)),
            pl.BlockSpec((1, 1, 1, IDX_HEADS * qb), lambda bi, qi: (bi, qi, 0, 0)),
            pl.BlockSpec((1, t, IDX_DIM), lambda bi, qi: (bi, 0, 0)),
            pl.BlockSpec((1, N_HEADS, qb, HEAD_DIM), lambda bi, qi: (bi, 0, qi, 0)),
            pl.BlockSpec((1, N_KV_HEADS, t, HEAD_DIM), lambda bi, qi: (bi, 0, 0, 0)),
            pl.BlockSpec((1, N_KV_HEADS, V_ROWS, t), lambda bi, qi: (bi, 0, 0, 0)),
        ],
        out_specs=pl.BlockSpec((1, qb, N_HEADS * HEAD_DIM), lambda bi, qi: (bi, qi, 0)),
        out_shape=jax.ShapeDtypeStruct((b, t, N_HEADS * HEAD_DIM), BF16),
        scratch_shapes=[
            pltpu.VMEM((t // kc, kc, qb), I32),
            pltpu.VMEM((t // kc, kc, qb), I16),
            pltpu.VMEM((t // kc, kc, qb), I16),
            pltpu.VMEM((N_KV_HEADS, 1, group * qb), F32),
            pltpu.VMEM((N_KV_HEADS, V_ROWS, group * qb), F32),
        ],
        compiler_params=_cparams("parallel", "arbitrary"),
        name="sparse_attention",
    )(iq, iw_rows, ik, q, k, vt)


def _pad_cols(w, mult):
    n = w.shape[1]
    pad = (-n) % mult
    return jnp.pad(w, ((0, 0), (0, pad))) if pad else w


def dsa_mixer(h2d, b, t, norm_gain, w_in, q_gain, k_gain, w_out):
    w = _pad_cols(w_in, LANES).astype(BF16)
    proj = norm_matmul(h2d, norm_gain, w, tm=1024, tn=w.shape[1] // 3)
    q, k, vt, iq, ik = attn_prep(proj, q_gain, k_gain, b, t, tt=256)
    iw_off = N_HEADS * HEAD_DIM + 2 * N_KV_HEADS * HEAD_DIM + IDX_HEADS * IDX_DIM + IDX_DIM
    iw = proj[:, iw_off:iw_off + IDX_HEADS].reshape(b, t, IDX_HEADS)
    o = sparse_attention(q, k, vt, iq, ik, iw, qb=128, kc=512)
    return matmul_residual(o.reshape(b * t, -1), w_out.astype(BF16), h2d, tm=1024, tn=512)


def _peer_q_body(x_ref, g_ref, w_ref, qt_ref, xnt_ref):
    @pl.when(pl.program_id(1) == 0)
    def _():
        x = x_ref[...]
        ms = jnp.mean(x * x, axis=-1, keepdims=True)
        xnt_ref[...] = (x * lax.rsqrt(ms + RMS_EPS) * g_ref[...]).T.astype(BF16)

    qt_ref[...] = jnp.dot(w_ref[...], xnt_ref[...], preferred_element_type=F32)


def peer_query(h2d, gain, w_qt, *, tm, tn):
    m, k = h2d.shape
    n = w_qt.shape[0]
    return pl.pallas_call(
        _peer_q_body,
        grid=(m // tm, n // tn),
        in_specs=[
            pl.BlockSpec((tm, k), lambda i, j: (i, 0)),
            pl.BlockSpec((1, k), lambda i, j: (0, 0)),
            pl.BlockSpec((tn, k), lambda i, j: (j, 0)),
        ],
        out_specs=[
            pl.BlockSpec((tn, tm), lambda i, j: (j, i)),
            pl.BlockSpec((k, tm), lambda i, j: (0, i)),
        ],
        out_shape=[jax.ShapeDtypeStruct((n, m), F32), jax.ShapeDtypeStruct((k, m), BF16)],
        compiler_params=_cparams("parallel", "arbitrary"),
        name="peer_query",
    )(h2d, gain.reshape(1, k), w_qt)


def _top_values(x, dst_ref, n):
    rank = jnp.full(x.shape, float(n + 1), F32)
    for r in range(n):
        m = x.max(axis=0, keepdims=True)
        dst_ref[r:r + 1, :] = m
        hit = x == m
        rank = jnp.where(hit, float(r + 1), rank)
        x = jnp.where(hit, -jnp.inf, x)
    return rank


def _peer_route_body(qt_ref, keys_ref, n_ref, e0_ref, rank_ref, e1_ref, a_ref, b_ref, cand_ref, f_ref):
    k = PEER_TOPK
    for h in range(PEER_HEADS):
        s, ranks = [], []
        for p, dst in ((0, a_ref), (1, b_ref)):
            r0 = (h * 2 + p) * PEER_NKEYS
            sp = jnp.dot(keys_ref[h, p], qt_ref[r0:r0 + PEER_NKEYS, :],
                         precision=lax.Precision.HIGHEST, preferred_element_type=F32)
            s.append(sp)
            ranks.append(_top_values(sp, dst, k))
        spans, off = [], 0
        for i in range(k):
            spans.append((off, k // (i + 1)))
            off += k // (i + 1)
        cand_ref[off - off % 8:, :] = jnp.full((cand_ref.shape[0] - off + off % 8, cand_ref.shape[1]), -jnp.inf, F32)
        for i, (o, n) in enumerate(spans):
            cand_ref[o:o + n, :] = a_ref[i:i + 1, :] + b_ref[0:n, :]
        _top_values(cand_ref[...], f_ref, k)
        f = f_ref[...]
        z = jnp.exp(f - f[0:1, :]).sum(axis=0, keepdims=True)
        tau = f[k - 1:k, :]
        partners = jnp.zeros(s[0].shape, F32)
        for i, (o, n) in enumerate(spans):
            cnt = jnp.where(cand_ref[o:o + n, :] >= tau, 1.0, 0.0).sum(axis=0, keepdims=True)
            partners = jnp.where(ranks[0] == float(i + 1), cnt, partners)
        n_ref[h] = partners
        e0_ref[h] = jnp.exp(s[0] - a_ref[0:1, :])
        rank_ref[h] = ranks[1].astype(rank_ref.dtype)
        e1_ref[h] = (jnp.exp(s[1] - b_ref[0:1, :]) / z).astype(e1_ref.dtype)


def peer_route(qt, keys, *, tt):
    n, m = qt.shape
    spec = pl.BlockSpec((PEER_HEADS, PEER_NKEYS, tt), lambda i: (0, 0, i))
    shp = jax.ShapeDtypeStruct((PEER_HEADS, PEER_NKEYS, m), F32)
    shp16 = jax.ShapeDtypeStruct((PEER_HEADS, PEER_NKEYS, m), BF16)
    n_cand = sum(PEER_TOPK // (i + 1) for i in range(PEER_TOPK))
    return pl.pallas_call(
        _peer_route_body,
        grid=(m // tt,),
        in_specs=[
            pl.BlockSpec((n, tt), lambda i: (0, i)),
            pl.BlockSpec(keys.shape, lambda i: (0, 0, 0, 0)),
        ],
        out_specs=[spec, spec, spec, spec],
        out_shape=[shp, shp, shp16, shp16],
        scratch_shapes=[
            pltpu.VMEM((PEER_TOPK, tt), F32),
            pltpu.VMEM((PEER_TOPK, tt), F32),
            pltpu.VMEM((-(-n_cand // 8) * 8, tt), F32),
            pltpu.VMEM((PEER_TOPK, tt), F32),
        ],
        compiler_params=_cparams("parallel"),
        name="peer_route",
    )(qt, keys)


def _gelu(x):
    return 0.5 * x * (1.0 + lax.erf(x * (2.0 ** -0.5)))


def _peer_expert_body(xnt_ref, u_ref, vt_ref, n_ref, e0_ref, rank_ref, e1_ref, h_ref, o_ref, acc_ref, coef_ref, *, rows):
    e = pl.program_id(1)

    @pl.when(e == 0)
    def _():
        acc_ref[...] = jnp.zeros(acc_ref.shape, F32)

    act = jnp.dot(u_ref[...], xnt_ref[...], preferred_element_type=F32)
    for ii in range(rows):
        w = None
        for hd in range(PEER_HEADS):
            npart = n_ref[hd, ii:ii + 1, :].astype(BF16)
            e0 = e0_ref[hd, ii:ii + 1, :].astype(BF16)
            t = jnp.where(rank_ref[hd] <= npart, e1_ref[hd], 0.0) * e0
            w = t if w is None else w + t
        sl = slice(ii * PEER_NKEYS, (ii + 1) * PEER_NKEYS)
        coef_ref[sl, :] = w * _gelu(act[sl, :]).astype(BF16)
    acc_ref[...] += jnp.dot(vt_ref[...], coef_ref[...], preferred_element_type=F32)

    @pl.when(e == pl.num_programs(1) - 1)
    def _():
        o_ref[...] = h_ref[...] + acc_ref[...].T


def peer_experts(xnt, u, vt, npart, e0, rank, e1, h2d, *, tt, te):
    d, m = xnt.shape
    n_exp = u.shape[0]
    rows = te // PEER_NKEYS
    row_spec = pl.BlockSpec((PEER_HEADS, rows, tt), lambda i, e: (0, e, i))
    col_spec = pl.BlockSpec((PEER_HEADS, PEER_NKEYS, tt), lambda i, e: (0, 0, i))
    tok_spec = pl.BlockSpec((tt, d), lambda i, e: (i, 0))
    return pl.pallas_call(
        functools.partial(_peer_expert_body, rows=rows),
        grid=(m // tt, n_exp // te),
        in_specs=[
            pl.BlockSpec((d, tt), lambda i, e: (0, i)),
            pl.BlockSpec((te, d), lambda i, e: (e, 0)),
            pl.BlockSpec((d, te), lambda i, e: (0, e)),
            row_spec, row_spec, col_spec, col_spec,
            tok_spec,
        ],
        out_specs=tok_spec,
        out_shape=jax.ShapeDtypeStruct((m, d), F32),
        scratch_shapes=[pltpu.VMEM((d, tt), F32), pltpu.VMEM((te, tt), BF16)],
        compiler_params=_cparams("parallel", "arbitrary"),
        name="peer_experts",
    )(xnt, u, vt, npart, e0, rank, e1, h2d)


def peer_mixer(h2d, gain, w_q, keys, u_tab, v_tab):
    qt, xnt = peer_query(h2d, gain, w_q.T.astype(BF16), tm=1024, tn=512)
    npart, e0, rank, e1 = peer_route(qt, keys, tt=256)
    return peer_experts(xnt, u_tab.astype(BF16), v_tab.T.astype(BF16), npart, e0, rank, e1, h2d, tt=512, te=1024)


def _ple_body(x_ref, g_ref, wg_ref, p_ref, wi_ref, r_ref, o_ref, xn_ref):
    @pl.when(pl.program_id(1) == 0)
    def _():
        x = x_ref[...]
        ms = jnp.mean(x * x, axis=-1, keepdims=True)
        xn_ref[...] = (x * lax.rsqrt(ms + RMS_EPS) * g_ref[...]).astype(BF16)

    gate = jax.nn.sigmoid(jnp.dot(xn_ref[...], wg_ref[...], preferred_element_type=F32))
    emb = jnp.dot(p_ref[...], wi_ref[...], preferred_element_type=F32)
    o_ref[...] = r_ref[...] + gate * emb


def ple_mixer(h2d, gain, w_gate, p2d, w_in, *, tm, tn):
    m, d = h2d.shape
    pd = p2d.shape[1]
    return pl.pallas_call(
        _ple_body,
        grid=(m // tm, d // tn),
        in_specs=[
            pl.BlockSpec((tm, d), lambda i, j: (i, 0)),
            pl.BlockSpec((1, d), lambda i, j: (0, 0)),
            pl.BlockSpec((d, tn), lambda i, j: (0, j)),
            pl.BlockSpec((tm, pd), lambda i, j: (i, 0)),
            pl.BlockSpec((pd, tn), lambda i, j: (0, j)),
            pl.BlockSpec((tm, tn), lambda i, j: (i, j)),
        ],
        out_specs=pl.BlockSpec((tm, tn), lambda i, j: (i, j)),
        out_shape=jax.ShapeDtypeStruct((m, d), F32),
        scratch_shapes=[pltpu.VMEM((tm, d), BF16)],
        compiler_params=_cparams("parallel", "arbitrary"),
        name="ple_mixer",
    )(h2d, gain.reshape(1, d), w_gate.astype(BF16), p2d.astype(BF16), w_in.astype(BF16), h2d)


def _dn_conv_body(x_ref, halo_ref, w_ref, *rest, mode, tt, hb):
    if mode == "k":
        o_ref, ot_ref, ext_ref = rest
    else:
        o_ref, ext_ref = rest
    width, d = DN_CONV_WIDTH, DN_HEAD_DIM
    first = pl.program_id(1) == 0
    ext_ref[0:8, :] = jnp.where(first, 0.0, halo_ref[...])
    ext_ref[8:, :] = x_ref[...]
    for h in range(hb):
        lanes = slice(h * d, (h + 1) * d)
        y = None
        for j in range(width):
            term = w_ref[j:j + 1, lanes] * ext_ref[pl.ds(8 - (width - 1) + j, tt), lanes]
            y = term if y is None else y + term
        y = y * jax.nn.sigmoid(y)
        if mode in ("q", "k"):
            y = y * lax.rsqrt(jnp.sum(y * y, axis=-1, keepdims=True) + RMS_EPS)
        if mode == "q":
            y = y * (d ** -0.5)
        o_ref[0, h] = y.astype(o_ref.dtype)
        if mode == "k":
            ot_ref[0, h] = y.T.astype(ot_ref.dtype)


def dn_conv_silu(proj, conv_w, b, t, *, mode, col0, heads, tt, hb):
    nt = t // tt
    d = DN_HEAD_DIM
    cb0 = col0 // (hb * d)
    out_spec = pl.BlockSpec((1, hb, tt, d), lambda bi, ti, hi: (bi, hi, ti, 0))
    out_shape = jax.ShapeDtypeStruct((b, heads, t, d), F32)
    out_specs, out_shapes = [out_spec], [out_shape]
    if mode == "k":
        out_specs.append(pl.BlockSpec((1, hb, d, tt), lambda bi, ti, hi: (bi, hi, 0, ti)))
        out_shapes.append(jax.ShapeDtypeStruct((b, heads, d, t), F32))
    res = pl.pallas_call(
        functools.partial(_dn_conv_body, mode=mode, tt=tt, hb=hb),
        grid=(b, nt, heads // hb),
        in_specs=[
            pl.BlockSpec((tt, hb * d), lambda bi, ti, hi: (bi * nt + ti, cb0 + hi)),
            pl.BlockSpec((8, hb * d), lambda bi, ti, hi: (jnp.maximum((bi * nt + ti) * (tt // 8) - 1, 0), cb0 + hi)),
            pl.BlockSpec((DN_CONV_WIDTH, hb * d), lambda bi, ti, hi: (0, cb0 + hi)),
        ],
        out_specs=out_specs,
        out_shape=out_shapes,
        scratch_shapes=[pltpu.VMEM((tt + 8, hb * d), F32)],
        compiler_params=_cparams("parallel", "parallel", "parallel"),
        name="dn_conv_" + mode,
    )(proj, proj, conv_w)
    return res if mode == "k" else res[0]


def _dn_gate_body(ba_ref, alog_ref, dtb_ref, beta_ref, gcum_ref, *, tm):
    nh = DN_V_HEADS
    beta_ref[...] = jax.nn.sigmoid(ba_ref[0:nh, :])
    a = ba_ref[nh:2 * nh, :]
    g = -jnp.exp(alog_ref[...]) * jax.nn.softplus(a + dtb_ref[...])
    lane = lax.broadcasted_iota(I32, (nh, LANES), 1) % DN_CHUNK
    for j in range(tm // LANES):
        x = g[:, j * LANES:(j + 1) * LANES]
        s = 1
        while s < DN_CHUNK:
            x = x + jnp.where(lane >= s, pltpu.roll(x, s, 1), 0.0)
            s *= 2
        gcum_ref[:, j * LANES:(j + 1) * LANES] = x


def dn_gates(ba_t, a_log, dt_bias, *, tm):
    m = ba_t.shape[1]
    nh = DN_V_HEADS
    spec = pl.BlockSpec((nh, tm), lambda i: (0, i))
    col = pl.BlockSpec((nh, 1), lambda i: (0, 0))
    return pl.pallas_call(
        functools.partial(_dn_gate_body, tm=tm),
        grid=(m // tm,),
        in_specs=[pl.BlockSpec((ba_t.shape[0], tm), lambda i: (0, i)), col, col],
        out_specs=[spec, spec],
        out_shape=[jax.ShapeDtypeStruct((nh, m), F32)] * 2,
        compiler_params=_cparams("parallel"),
        name="dn_gates",
    )(ba_t, a_log.reshape(nh, 1), dt_bias.reshape(nh, 1))


def _dn_local_body(q_ref, k_ref, kt_ref, v_ref, gc_ref, gr_ref, bc_ref, u_ref, l1_ref, l2_ref, eg_ref, *, gsz, nchunk):
    c, d = DN_CHUNK, DN_HEAD_DIM
    rep = DN_V_HEADS // DN_QK_HEADS
    row = lax.broadcasted_iota(I32, (c, c), 0)
    colm = lax.broadcasted_iota(I32, (c, c), 1)
    tril, strict = row >= colm, row > colm
    eye = jnp.where(row == colm, 1.0, 0.0)
    bodies = [(cc, jv) for cc in range(nchunk) for jv in range(gsz)]

    kk, qk = {}, {}
    for cc in range(nchunk):
        rows = slice(cc * c, (cc + 1) * c)
        for jq in range(gsz // rep):
            kb = k_ref[0, jq, rows, :].astype(BF16)
            qb = q_ref[0, jq, rows, :].astype(BF16)
            kk[cc, jq] = lax.dot_general(kb, kb, _NT, preferred_element_type=F32)
            qk[cc, jq] = lax.dot_general(qb, kb, _NT, preferred_element_type=F32)

    x, inv, rhs = {}, {}, {}
    for cc, jv in bodies:
        rows = slice(cc * c, (cc + 1) * c)
        jq = jv // rep
        gc = gc_ref[0, 0, rows, jv:jv + 1]
        gr = gr_ref[0, 0, jv:jv + 1, rows]
        beta = bc_ref[0, 0, rows, jv:jv + 1]
        decay = jnp.exp(jnp.where(tril, gc - gr, NEG_BIG))
        a = jnp.where(strict, kk[cc, jq] * decay, 0.0) * beta
        x[cc, jv] = a
        inv[cc, jv] = eye - a
        eg = jnp.exp(gc)
        k = k_ref[0, jq, rows, :]
        rhs[cc, jv] = jnp.concatenate([v_ref[0, jv, rows, :] * beta, k * (beta * eg)], axis=1).astype(BF16)
        g_last = gc[c - 1:c, :]
        kd_t = kt_ref[0, jq, :, rows] * jnp.exp(g_last - gr)
        l2_ref[0, jv, cc] = jnp.concatenate([qk[cc, jq] * decay, kd_t], axis=0).astype(l2_ref.dtype)
        l1_ref[0, jv, cc, c:2 * c, :] = (q_ref[0, jq, rows, :] * eg).astype(l1_ref.dtype)
        eg_ref[0, jv, cc] = jnp.broadcast_to(jnp.exp(g_last), (1, d))

    span = 2
    while span < c:
        for key in bodies:
            xb = x[key].astype(BF16)
            x[key] = jnp.dot(xb, xb, preferred_element_type=F32)
        for key in bodies:
            inv[key] = inv[key] + jnp.dot(inv[key].astype(BF16), x[key].astype(BF16), preferred_element_type=F32)
        span *= 2

    for cc, jv in bodies:
        rows = slice(cc * c, (cc + 1) * c)
        sol = jnp.dot(inv[cc, jv].astype(BF16), rhs[cc, jv], preferred_element_type=F32)
        u_ref[0, jv, rows, :] = sol[:, :d]
        l1_ref[0, jv, cc, 0:c, :] = sol[:, d:].astype(l1_ref.dtype)


def dn_local(q, k, kt, v, gc_col, gc_row, beta_col, *, gsz, tb):
    b, hv, t, d = v.shape
    c = DN_CHUNK
    rep = DN_V_HEADS // DN_QK_HEADS
    nb = t // tb
    nchunk = tb // c
    qk_spec = pl.BlockSpec((1, gsz // rep, tb, d), lambda bi, gi, ni: (bi, gi, ni, 0))
    col_spec = pl.BlockSpec((1, 1, tb, gsz), lambda bi, gi, ni: (bi, gi, ni, 0))

    def chunked(rows, lanes):
        return pl.BlockSpec((1, gsz, nchunk, rows, lanes), lambda bi, gi, ni: (bi, gi, ni, 0, 0))

    return pl.pallas_call(
        functools.partial(_dn_local_body, gsz=gsz, nchunk=nchunk),
        grid=(b, hv // gsz, nb),
        in_specs=[
            qk_spec, qk_spec,
            pl.BlockSpec((1, gsz // rep, d, tb), lambda bi, gi, ni: (bi, gi, 0, ni)),
            pl.BlockSpec((1, gsz, tb, d), lambda bi, gi, ni: (bi, gi, ni, 0)),
            col_spec,
            pl.BlockSpec((1, 1, gsz, tb), lambda bi, gi, ni: (bi, gi, 0, ni)),
            col_spec,
        ],
        out_specs=[
            pl.BlockSpec((1, gsz, tb, d), lambda bi, gi, ni: (bi, gi, ni, 0)),
            chunked(2 * c, d), chunked(c + d, c), chunked(1, d),
        ],
        out_shape=[
            jax.ShapeDtypeStruct((b, hv, t, d), F32),
            jax.ShapeDtypeStruct((b, hv, t // c, 2 * c, d), BF16),
            jax.ShapeDtypeStruct((b, hv, t // c, c + d, c), BF16),
            jax.ShapeDtypeStruct((b, hv, t // c, 1, d), F32),
        ],
        compiler_params=_cparams("parallel", "parallel", "parallel"),
        name="dn_local",
    )(q, k, kt, v, gc_col, gc_row, beta_col)


def _dn_scan_body(u_ref, l1_ref, l2_ref, eg_ref, z_ref, gain_ref, o_ref, state_ref, *, gsz, nchunk):
    c, d = DN_CHUNK, DN_HEAD_DIM

    @pl.when(pl.program_id(2) == 0)
    def _():
        state_ref[...] = jnp.zeros(state_ref.shape, F32)

    gain = gain_ref[...]
    heads = range(gsz)
    for cc in range(nchunk):
        rows = slice(cc * c, (cc + 1) * c)
        r1 = [jnp.dot(l1_ref[0, jv, cc], state_ref[jv].astype(BF16), preferred_element_type=F32) for jv in heads]
        v_new = [(u_ref[0, jv, rows, :] - r1[jv][:c]).astype(BF16) for jv in heads]
        r2 = [jnp.dot(l2_ref[0, jv, cc], v_new[jv], preferred_element_type=F32) for jv in heads]
        for jv in heads:
            state_ref[jv] = state_ref[jv] * eg_ref[0, jv, cc] + r2[jv][c:]
        for jv in heads:
            o = r1[jv][c:] + r2[jv][:c]
            ms = jnp.mean(o * o, axis=-1, keepdims=True)
            z = z_ref[rows, jv * d:(jv + 1) * d]
            o_ref[rows, jv * d:(jv + 1) * d] = (
                o * lax.rsqrt(ms + RMS_EPS) * gain * (z * jax.nn.sigmoid(z))).astype(o_ref.dtype)


def dn_scan(u, l1, l2, eg, proj, z_col0, norm_gain, *, gsz, tb):
    b, hv, t, d = u.shape
    c = DN_CHUNK
    nb = t // tb
    nchunk = tb // c
    zb0 = z_col0 // (gsz * d)

    def chunked(rows, lanes):
        return pl.BlockSpec((1, gsz, nchunk, rows, lanes), lambda bi, gi, ni: (bi, gi, ni, 0, 0))

    return pl.pallas_call(
        functools.partial(_dn_scan_body, gsz=gsz, nchunk=nchunk),
        grid=(b, hv // gsz, nb),
        in_specs=[
            pl.BlockSpec((1, gsz, tb, d), lambda bi, gi, ni: (bi, gi, ni, 0)),
            chunked(2 * c, d), chunked(c + d, c), chunked(1, d),
            pl.BlockSpec((tb, gsz * d), lambda bi, gi, ni: (bi * nb + ni, zb0 + gi)),
            pl.BlockSpec((1, d), lambda bi, gi, ni: (0, 0)),
        ],
        out_specs=pl.BlockSpec((tb, gsz * d), lambda bi, gi, ni: (bi * nb + ni, gi)),
        out_shape=jax.ShapeDtypeStruct((b * t, hv * d), BF16),
        scratch_shapes=[pltpu.VMEM((gsz, d, d), F32)],
        compiler_params=_cparams("parallel", "parallel", "arbitrary"),
        name="dn_scan",
    )(u, l1, l2, eg, proj, norm_gain.reshape(1, d))


def gdn_mixer(h2d, b, t, norm_gain, w_in, conv_w, a_log, dt_bias, out_gain, w_out):
    key_dim = DN_QK_HEADS * DN_HEAD_DIM
    val_dim = DN_V_HEADS * DN_HEAD_DIM
    main = 2 * key_dim + 2 * val_dim
    proj = norm_matmul(h2d, norm_gain, w_in[:, :main].astype(BF16), tm=1024, tn=1024)
    w_ba_t = _pad_cols(w_in[:, main:], LANES).T.astype(BF16)
    ba_t, _ = peer_query(h2d, norm_gain, w_ba_t, tm=512, tn=LANES)
    beta_t, gcum_t = dn_gates(ba_t, a_log, dt_bias, tm=512)
    q = dn_conv_silu(proj, conv_w, b, t, mode="q", col0=0, heads=DN_QK_HEADS, tt=512, hb=8)
    k, kt = dn_conv_silu(proj, conv_w, b, t, mode="k", col0=key_dim, heads=DN_QK_HEADS, tt=512, hb=8)
    v = dn_conv_silu(proj, conv_w, b, t, mode="v", col0=2 * key_dim, heads=DN_V_HEADS, tt=512, hb=8)
    gsz = 8
    hg = DN_V_HEADS // gsz
    rows = lambda x: x.reshape(hg, gsz, b, t).transpose(2, 0, 1, 3)
    cols = lambda x: rows(x).transpose(0, 1, 3, 2)
    u, l1, l2, eg = dn_local(q, k, kt, v, cols(gcum_t), rows(gcum_t), cols(beta_t), gsz=gsz, tb=2 * DN_CHUNK)
    o = dn_scan(u, l1, l2, eg, proj, 2 * key_dim + val_dim, out_gain, gsz=gsz, tb=4 * DN_CHUNK)
    return matmul_residual(o, w_out.astype(BF16), h2d, tm=1024, tn=512)


def kernel(x, p, norm_mix, norm_ffn, norm_ple, attn_w_in, attn_q_norm, attn_k_norm, attn_w_out, dn_w_in, dn_conv,
           dn_a_log, dn_dt_bias, dn_norm, dn_w_out, peer_w_q, peer_keys, peer_u, peer_v, ple_w_in, ple_w_gate):
    b, t, d = x.shape
    n_mixers = 2
    h = x.reshape(b * t, d)
    for i in range(p.shape[0]):
        j = i // n_mixers
        if i % n_mixers == 0:
            h = dsa_mixer(h, b, t, norm_mix[i], attn_w_in[j], attn_q_norm[j], attn_k_norm[j], attn_w_out[j])
        else:
            h = gdn_mixer(h, b, t, norm_mix[i], dn_w_in[j], dn_conv[j], dn_a_log[j], dn_dt_bias[j], dn_norm[j],
                          dn_w_out[j])
        h = peer_mixer(h, norm_ffn[i], peer_w_q[i], peer_keys[i], peer_u[i], peer_v[i])
        h = ple_mixer(h, norm_ple[i], ple_w_gate[i], p[i].reshape(b * t, -1), ple_w_in[i], tm=1024, tn=512)
    return h.reshape(b, t, d)
```

```python
import functools

import jax
import jax.numpy as jnp
import numpy as np
from jax import lax
from jax.experimental import pallas as pl
from jax.experimental.pallas import tpu as pltpu

F32 = jnp.float32
BF16 = jnp.bfloat16
I32 = jnp.int32
I16 = jnp.int16
HALF16 = 2 ** 15

RMS_EPS = 1e-6
ROPE_THETA = 500000.0
ROT_FRACTION = 4

N_HEADS = 16
N_KV_HEADS = 4
HEAD_DIM = 128
IDX_HEADS = 16
IDX_DIM = 64
INDEX_TOPK = 256
V_ROWS = HEAD_DIM + 16

DN_QK_HEADS = 16
DN_V_HEADS = 32
DN_HEAD_DIM = 128
DN_CONV_WIDTH = 4
DN_CHUNK = 64

PEER_HEADS = 8
PEER_NKEYS = 128
PEER_TOPK = 16

LANES = 128
VMEM_LIMIT = 56 * 1024 * 1024

LOG2E = 1.4426950408889634
INT_MIN = -(2 ** 31)
NEG_BIG = -1e30

_NT = (((1,), (1,)), ((), ()))


def _cparams(*sem):
    return pltpu.CompilerParams(dimension_semantics=sem, vmem_limit_bytes=VMEM_LIMIT)


def _norm_mm_body(x_ref, g_ref, w_ref, o_ref, xn_ref):
    @pl.when(pl.program_id(1) == 0)
    def _():
        x = x_ref[...]
        ms = jnp.mean(x * x, axis=-1, keepdims=True)
        xn_ref[...] = (x * lax.rsqrt(ms + RMS_EPS) * g_ref[...]).astype(BF16)

    o_ref[...] = jnp.dot(xn_ref[...], w_ref[...], preferred_element_type=F32).astype(o_ref.dtype)


def norm_matmul(x, gain, w, *, tm, tn, out_dtype=F32):
    m, k = x.shape
    n = w.shape[1]
    return pl.pallas_call(
        _norm_mm_body,
        grid=(m // tm, n // tn),
        in_specs=[
            pl.BlockSpec((tm, k), lambda i, j: (i, 0)),
            pl.BlockSpec((1, k), lambda i, j: (0, 0)),
            pl.BlockSpec((k, tn), lambda i, j: (0, j)),
        ],
        out_specs=pl.BlockSpec((tm, tn), lambda i, j: (i, j)),
        out_shape=jax.ShapeDtypeStruct((m, n), out_dtype),
        scratch_shapes=[pltpu.VMEM((tm, k), BF16)],
        compiler_params=_cparams("parallel", "arbitrary"),
        name="norm_matmul",
    )(x, gain.reshape(1, k), w)


def _mm_res_body(x_ref, w_ref, r_ref, o_ref):
    o_ref[...] = r_ref[...] + jnp.dot(x_ref[...], w_ref[...], preferred_element_type=F32)


def matmul_residual(x, w, res, *, tm, tn):
    m, k = x.shape
    n = w.shape[1]
    return pl.pallas_call(
        _mm_res_body,
        grid=(m // tm, n // tn),
        in_specs=[
            pl.BlockSpec((tm, k), lambda i, j: (i, 0)),
            pl.BlockSpec((k, tn), lambda i, j: (0, j)),
            pl.BlockSpec((tm, tn), lambda i, j: (i, j)),
        ],
        out_specs=pl.BlockSpec((tm, tn), lambda i, j: (i, j)),
        out_shape=jax.ShapeDtypeStruct((m, n), F32),
        compiler_params=_cparams("parallel", "arbitrary"),
        name="matmul_residual",
    )(x, w, res)


def _rope_tables(t, head_dim):
    rot = head_dim // ROT_FRACTION
    half = rot // 2
    inv_freq = ROPE_THETA ** (-jnp.arange(half, dtype=F32) * (2.0 / rot))
    ang = jnp.arange(t, dtype=jnp.int32).astype(F32)[:, None] * inv_freq[None, :]
    cos, sin = jnp.cos(ang), jnp.sin(ang)
    rest = head_dim - rot
    c = jnp.concatenate([cos, cos, jnp.ones((t, rest), F32)], axis=-1)
    s_lo = jnp.concatenate([-sin, jnp.zeros((t, half + rest), F32)], axis=-1)
    s_hi = jnp.concatenate([jnp.zeros((t, half), F32), sin, jnp.zeros((t, rest), F32)], axis=-1)
    reps = LANES // head_dim
    return tuple(jnp.tile(a, (1, reps)) for a in (c, s_lo, s_hi)), half


def _rope_tile(x, c, s_lo, s_hi, half):
    return x * c + pltpu.roll(x, LANES - half, 1) * s_lo + pltpu.roll(x, half, 1) * s_hi


def _attn_prep_body(p_ref, qg_ref, kg_ref, c128_ref, sl128_ref, sh128_ref, c64_ref, sl64_ref, sh64_ref,
                    q_ref, k_ref, vt_ref, iq_ref, ik_ref, *, half128, half64, qb):
    c128, sl128, sh128 = c128_ref[...], sl128_ref[...], sh128_ref[...]
    c64, sl64, sh64 = c64_ref[...], sl64_ref[...], sh64_ref[...]

    def normed(x, g):
        ms = jnp.mean(x * x, axis=-1, keepdims=True)
        return x * lax.rsqrt(ms + RMS_EPS) * g

    off = 0
    nblk = p_ref.shape[0] // qb
    group = N_HEADS // N_KV_HEADS
    for h in range(N_HEADS):
        x = normed(p_ref[:, off:off + HEAD_DIM], qg_ref[...])
        xt = (_rope_tile(x, c128, sl128, sh128, half128) * (HEAD_DIM ** -0.5 * LOG2E)).T.astype(BF16)
        n, g = divmod(h, group)
        for j in range(nblk):
            c0 = (j * group + g) * qb
            q_ref[0, n, :, c0:c0 + qb] = xt[:, j * qb:(j + 1) * qb]
        off += HEAD_DIM
    for h in range(N_KV_HEADS):
        x = normed(p_ref[:, off:off + HEAD_DIM], kg_ref[...])
        k_ref[0, h] = _rope_tile(x, c128, sl128, sh128, half128).astype(BF16)
        off += HEAD_DIM
    ones_rows = jnp.ones((V_ROWS - HEAD_DIM, p_ref.shape[0]), BF16)
    for h in range(N_KV_HEADS):
        vt_ref[0, h, 0:HEAD_DIM, :] = p_ref[:, off:off + HEAD_DIM].T.astype(BF16)
        vt_ref[0, h, HEAD_DIM:V_ROWS, :] = ones_rows
        off += HEAD_DIM
    for j in range(IDX_HEADS * IDX_DIM // LANES):
        xt = (_rope_tile(p_ref[:, off:off + LANES], c64, sl64, sh64, half64) * (IDX_DIM ** -0.5)).T.astype(BF16)
        for u in range(LANES // IDX_DIM):
            h = j * (LANES // IDX_DIM) + u
            for jb in range(nblk):
                c0 = (jb * IDX_HEADS + h) * qb
                iq_ref[0, :, c0:c0 + qb] = xt[u * IDX_DIM:(u + 1) * IDX_DIM, jb * qb:(jb + 1) * qb]
        off += LANES
    x = _rope_tile(p_ref[:, off:off + LANES], c64, sl64, sh64, half64)
    ik_ref[0] = x[:, :IDX_DIM].astype(BF16)


def attn_prep(proj, q_gain, k_gain, b, t, *, tt, qb):
    (c128, sl128, sh128), half128 = _rope_tables(t, HEAD_DIM)
    (c64, sl64, sh64), half64 = _rope_tables(t, IDX_DIM)
    nt = t // tt
    width = proj.shape[1]
    group = N_HEADS // N_KV_HEADS
    tab = pl.BlockSpec((tt, LANES), lambda bi, ti: (ti, 0))
    gain = pl.BlockSpec((1, HEAD_DIM), lambda bi, ti: (0, 0))
    return pl.pallas_call(
        functools.partial(_attn_prep_body, half128=half128, half64=half64, qb=qb),
        grid=(b, nt),
        in_specs=[pl.BlockSpec((tt, width), lambda bi, ti: (bi * nt + ti, 0)), gain, gain,
                  tab, tab, tab, tab, tab, tab],
        out_specs=[
            pl.BlockSpec((1, N_KV_HEADS, HEAD_DIM, group * tt), lambda bi, ti: (bi, 0, 0, ti)),
            pl.BlockSpec((1, N_KV_HEADS, tt, HEAD_DIM), lambda bi, ti: (bi, 0, ti, 0)),
            pl.BlockSpec((1, N_KV_HEADS, V_ROWS, tt), lambda bi, ti: (bi, 0, 0, ti)),
            pl.BlockSpec((1, IDX_DIM, IDX_HEADS * tt), lambda bi, ti: (bi, 0, ti)),
            pl.BlockSpec((1, tt, IDX_DIM), lambda bi, ti: (bi, ti, 0)),
        ],
        out_shape=[
            jax.ShapeDtypeStruct((b, N_KV_HEADS, HEAD_DIM, group * t), BF16),
            jax.ShapeDtypeStruct((b, N_KV_HEADS, t, HEAD_DIM), BF16),
            jax.ShapeDtypeStruct((b, N_KV_HEADS, V_ROWS, t), BF16),
            jax.ShapeDtypeStruct((b, IDX_DIM, IDX_HEADS * t), BF16),
            jax.ShapeDtypeStruct((b, t, IDX_DIM), BF16),
        ],
        compiler_params=_cparams("parallel", "parallel"),
        name="attn_prep",
    )(proj, q_gain.reshape(1, HEAD_DIM), k_gain.reshape(1, HEAD_DIM), c128, sl128, sh128, c64, sl64, sh64)


def _attn_body(iq_ref, iw_ref, ik_ref, q_ref, k_ref, vt_ref, o_ref,
               keys_ref, hi_ref, lo_ref, m_ref, acc_ref, *, qb, kc, topk):
    qi = pl.program_id(1)
    group = N_HEADS // N_KV_HEADS
    nck = ((qi + 1) * qb + kc - 1) // kc
    iq = iq_ref[0]
    iw = iw_ref[0, 0] * (IDX_HEADS ** -0.5)
    qpos = qi * qb + lax.broadcasted_iota(I32, (kc, qb), 1)
    krow = lax.broadcasted_iota(I32, (kc, qb), 0)

    def score_chunk(c, carry):
        off = pl.multiple_of(c * kc, kc)
        ikc = ik_ref[0, pl.ds(off, kc), :]
        s = jnp.dot(ikc, iq, preferred_element_type=F32)
        s = jnp.maximum(s, 0.0) * iw
        sc = s[:, 0:qb]
        for h in range(1, IDX_HEADS):
            sc = sc + s[:, h * qb:(h + 1) * qb]
        bits = pltpu.bitcast(sc, I32)
        key = bits ^ ((bits >> 31) & 0x7FFFFFFF)
        key = jnp.where(krow + off <= qpos, key, INT_MIN)
        keys_ref[c] = key
        hi_ref[c] = (key >> 16).astype(I16)
        lo_ref[c] = ((key & 0xFFFF) - HALF16).astype(I16)
        return carry

    lax.fori_loop(0, nck, score_chunk, 0)

    one16, zero16 = jnp.ones((), I16), jnp.zeros((), I16)

    def count_ge(ref, cand):
        cand16 = jnp.broadcast_to(cand.astype(I16), (16, qb))

        def body(c, acc):
            parts = [jnp.where(ref[c, r * 16:(r + 1) * 16, :] >= cand16, one16, zero16) for r in range(kc // 16)]
            while len(parts) > 1:
                parts = [parts[i] + parts[i + 1] for i in range(0, len(parts), 2)]
            return acc + parts[0].astype(I32)

        acc = lax.fori_loop(0, nck, body, jnp.zeros((16, qb), I32))
        return acc.sum(axis=0, keepdims=True)

    def largest_with_count(ref, need):
        zero = jnp.zeros((1, qb), I32)
        ans = jnp.where(count_ge(ref, zero) >= need, zero, -HALF16)

        def bit_step(i, ans):
            cand = ans | lax.shift_left(jnp.int32(1), 14 - i)
            return jnp.where(count_ge(ref, cand) >= need, cand, ans)

        return lax.fori_loop(0, 15, bit_step, ans)

    ans_hi = largest_with_count(hi_ref, topk)
    above = jnp.where(ans_hi == HALF16 - 1, 0, count_ge(hi_ref, jnp.minimum(ans_hi + 1, HALF16 - 1)))
    hi16 = jnp.broadcast_to(ans_hi.astype(I16), (kc, qb))

    def mask_low(c, carry):
        lo_ref[c] = jnp.where(hi_ref[c] == hi16, lo_ref[c], jnp.full((), -HALF16, I16))
        return carry

    lax.fori_loop(0, nck, mask_low, 0)
    ans_lo = largest_with_count(lo_ref, topk - above)
    tau = lax.shift_left(ans_hi, 16) | (ans_lo + HALF16)
    tau = jnp.maximum(tau, INT_MIN + 1)

    m_ref[...] = jnp.full(m_ref.shape, NEG_BIG, F32)
    acc_ref[...] = jnp.zeros(acc_ref.shape, F32)

    def attn_chunk(c, carry):
        off = pl.multiple_of(c * kc, kc)
        bias = jnp.where(keys_ref[c] >= tau, 0.0, NEG_BIG)
        bias = jnp.concatenate([bias] * group, axis=1)
        heads = range(N_KV_HEADS)
        s = [jnp.dot(k_ref[0, n, pl.ds(off, kc), :], q_ref[0, n], preferred_element_type=F32) + bias
             for n in heads]
        m_new = [jnp.maximum(m_ref[n], s[n].max(axis=0, keepdims=True)) for n in heads]
        p = [jnp.exp2(s[n] - m_new[n]) for n in heads]
        pv = [jnp.dot(vt_ref[0, n, :, pl.ds(off, kc)], p[n].astype(BF16), preferred_element_type=F32)
              for n in heads]
        for n in heads:
            acc_ref[n] = jnp.exp2(m_ref[n] - m_new[n]) * acc_ref[n] + pv[n]
            m_ref[n] = m_new[n]
        return carry

    lax.fori_loop(0, nck, attn_chunk, 0)

    for n in range(N_KV_HEADS):
        on = acc_ref[n, 0:HEAD_DIM, :] / acc_ref[n, HEAD_DIM:HEAD_DIM + 1, :]
        for g in range(group):
            h = n * group + g
            o_ref[0, :, h * HEAD_DIM:(h + 1) * HEAD_DIM] = on[:, g * qb:(g + 1) * qb].T.astype(o_ref.dtype)


def sparse_attention(q, k, vt, iq, ik, iw, *, qb, kc):
    b, _, t, _ = k.shape
    nq = t // qb
    topk = min(INDEX_TOPK, t // 4)
    group = N_HEADS // N_KV_HEADS
    iw_rows = iw.reshape(b, nq, qb, IDX_HEADS).transpose(0, 1, 3, 2).reshape(b, nq, 1, IDX_HEADS * qb)
    return pl.pallas_call(
        functools.partial(_attn_body, qb=qb, kc=kc, topk=topk),
        grid=(b, nq),
        in_specs=[
            pl.BlockSpec((1, IDX_DIM, IDX_HEADS * qb), lambda bi, qi: (bi, 0, qi)),
            pl.BlockSpec((1, 1, 1, IDX_HEADS * qb), lambda bi, qi: (bi, qi, 0, 0)),
            pl.BlockSpec((1, t, IDX_DIM), lambda bi, qi: (bi, 0, 0)),
            pl.BlockSpec((1, N_KV_HEADS, HEAD_DIM, group * qb), lambda bi, qi: (bi, 0, 0, qi)),
            pl.BlockSpec((1, N_KV_HEADS, t, HEAD_DIM), lambda bi, qi: (bi, 0, 0, 0)),
            pl.BlockSpec((1, N_KV_HEADS, V_ROWS, t), lambda bi, qi: (bi, 0, 0, 0)),
        ],
        out_specs=pl.BlockSpec((1, qb, N_HEADS * HEAD_DIM), lambda bi, qi: (bi, qi, 0)),
        out_shape=jax.ShapeDtypeStruct((b, t, N_HEADS * HEAD_DIM), BF16),
        scratch_shapes=[
            pltpu.VMEM((t // kc, kc, qb), I32),
            pltpu.VMEM((t // kc, kc, qb), I16),
            pltpu.VMEM((t // kc, kc, qb), I16),
            pltpu.VMEM((N_KV_HEADS, 1, group * qb), F32),
            pltpu.VMEM((N_KV_HEADS, V_ROWS, group * qb), F32),
        ],
        compiler_params=_cparams("parallel", "arbitrary"),
        name="sparse_attention",
    )(iq, iw_rows, ik, q, k, vt)


def _pad_cols(w, mult):
    n = w.shape[1]
    pad = (-n) % mult
    return jnp.pad(w, ((0, 0), (0, pad))) if pad else w


def dsa_mixer(h2d, b, t, norm_gain, w_in, q_gain, k_gain, w_out):
    w = _pad_cols(w_in, LANES).astype(BF16)
    proj = norm_matmul(h2d, norm_gain, w, tm=1024, tn=w.shape[1] // 3)
    q, k, vt, iq, ik = attn_prep(proj, q_gain, k_gain, b, t, tt=256, qb=128)
    iw_off = N_HEADS * HEAD_DIM + 2 * N_KV_HEADS * HEAD_DIM + IDX_HEADS * IDX_DIM + IDX_DIM
    iw = proj[:, iw_off:iw_off + IDX_HEADS].reshape(b, t, IDX_HEADS)
    o = sparse_attention(q, k, vt, iq, ik, iw, qb=128, kc=512)
    return matmul_residual(o.reshape(b * t, -1), w_out.astype(BF16), h2d, tm=1024, tn=512)


def _peer_q_body(x_ref, g_ref, w_ref, qt_ref, xnt_ref):
    @pl.when(pl.program_id(1) == 0)
    def _():
        x = x_ref[...]
        ms = jnp.mean(x * x, axis=-1, keepdims=True)
        xnt_ref[...] = (x * lax.rsqrt(ms + RMS_EPS) * g_ref[...]).T.astype(BF16)

    qt_ref[...] = jnp.dot(w_ref[...], xnt_ref[...], preferred_element_type=F32)


def peer_query(h2d, gain, w_qt, *, tm, tn):
    m, k = h2d.shape
    n = w_qt.shape[0]
    return pl.pallas_call(
        _peer_q_body,
        grid=(m // tm, n // tn),
        in_specs=[
            pl.BlockSpec((tm, k), lambda i, j: (i, 0)),
            pl.BlockSpec((1, k), lambda i, j: (0, 0)),
            pl.BlockSpec((tn, k), lambda i, j: (j, 0)),
        ],
        out_specs=[
            pl.BlockSpec((tn, tm), lambda i, j: (j, i)),
            pl.BlockSpec((k, tm), lambda i, j: (0, i)),
        ],
        out_shape=[jax.ShapeDtypeStruct((n, m), F32), jax.ShapeDtypeStruct((k, m), BF16)],
        compiler_params=_cparams("parallel", "arbitrary"),
        name="peer_query",
    )(h2d, gain.reshape(1, k), w_qt)


def _top_values(x, dst_ref, n):
    rank = jnp.full(x.shape, float(n + 1), F32)
    for r in range(n):
        m = x.max(axis=0, keepdims=True)
        dst_ref[r:r + 1, :] = m
        hit = x == m
        rank = jnp.where(hit, float(r + 1), rank)
        x = jnp.where(hit, -jnp.inf, x)
    return rank


def _peer_route_body(qt_ref, keys_ref, n_ref, e0_ref, rank_ref, e1_ref, a_ref, b_ref, cand_ref, f_ref):
    k = PEER_TOPK
    for h in range(PEER_HEADS):
        s, ranks = [], []
        for p, dst in ((0, a_ref), (1, b_ref)):
            r0 = (h * 2 + p) * PEER_NKEYS
            sp = jnp.dot(keys_ref[h, p], qt_ref[r0:r0 + PEER_NKEYS, :],
                         precision=lax.Precision.HIGHEST, preferred_element_type=F32)
            s.append(sp)
            ranks.append(_top_values(sp, dst, k))
        spans, off = [], 0
        for i in range(k):
            spans.append((off, k // (i + 1)))
            off += k // (i + 1)
        cand_ref[off - off % 8:, :] = jnp.full((cand_ref.shape[0] - off + off % 8, cand_ref.shape[1]), -jnp.inf, F32)
        for i, (o, n) in enumerate(spans):
            cand_ref[o:o + n, :] = a_ref[i:i + 1, :] + b_ref[0:n, :]
        _top_values(cand_ref[...], f_ref, k)
        f = f_ref[...]
        z = jnp.exp(f - f[0:1, :]).sum(axis=0, keepdims=True)
        tau = f[k - 1:k, :]
        partners = jnp.zeros(s[0].shape, F32)
        for i, (o, n) in enumerate(spans):
            cnt = jnp.where(cand_ref[o:o + n, :] >= tau, 1.0, 0.0).sum(axis=0, keepdims=True)
            partners = jnp.where(ranks[0] == float(i + 1), cnt, partners)
        n_ref[h] = partners
        e0_ref[h] = jnp.exp(s[0] - a_ref[0:1, :])
        rank_ref[h] = ranks[1].astype(rank_ref.dtype)
        e1_ref[h] = (jnp.exp(s[1] - b_ref[0:1, :]) / z).astype(e1_ref.dtype)


def peer_route(qt, keys, *, tt):
    n, m = qt.shape
    spec = pl.BlockSpec((PEER_HEADS, PEER_NKEYS, tt), lambda i: (0, 0, i))
    shp = jax.ShapeDtypeStruct((PEER_HEADS, PEER_NKEYS, m), F32)
    shp16 = jax.ShapeDtypeStruct((PEER_HEADS, PEER_NKEYS, m), BF16)
    n_cand = sum(PEER_TOPK // (i + 1) for i in range(PEER_TOPK))
    return pl.pallas_call(
        _peer_route_body,
        grid=(m // tt,),
        in_specs=[
            pl.BlockSpec((n, tt), lambda i: (0, i)),
            pl.BlockSpec(keys.shape, lambda i: (0, 0, 0, 0)),
        ],
        out_specs=[spec, spec, spec, spec],
        out_shape=[shp, shp, shp16, shp16],
        scratch_shapes=[
            pltpu.VMEM((PEER_TOPK, tt), F32),
            pltpu.VMEM((PEER_TOPK, tt), F32),
            pltpu.VMEM((-(-n_cand // 8) * 8, tt), F32),
            pltpu.VMEM((PEER_TOPK, tt), F32),
        ],
        compiler_params=_cparams("parallel"),
        name="peer_route",
    )(qt, keys)


def _gelu(x):
    return 0.5 * x * (1.0 + lax.erf(x * (2.0 ** -0.5)))


def _peer_expert_body(xnt_ref, u_ref, vt_ref, n_ref, e0_ref, rank_ref, e1_ref, h_ref, o_ref, acc_ref, coef_ref, *, rows):
    e = pl.program_id(1)

    @pl.when(e == 0)
    def _():
        acc_ref[...] = jnp.zeros(acc_ref.shape, F32)

    act = jnp.dot(u_ref[...], xnt_ref[...], preferred_element_type=F32)
    for ii in range(rows):
        w = None
        for hd in range(PEER_HEADS):
            npart = n_ref[hd, ii:ii + 1, :].astype(BF16)
            e0 = e0_ref[hd, ii:ii + 1, :].astype(BF16)
            t = jnp.where(rank_ref[hd] <= npart, e1_ref[hd], 0.0) * e0
            w = t if w is None else w + t
        sl = slice(ii * PEER_NKEYS, (ii + 1) * PEER_NKEYS)
        coef_ref[sl, :] = w * _gelu(act[sl, :]).astype(BF16)
    acc_ref[...] += jnp.dot(vt_ref[...], coef_ref[...], preferred_element_type=F32)

    @pl.when(e == pl.num_programs(1) - 1)
    def _():
        o_ref[...] = h_ref[...] + acc_ref[...].T


def peer_experts(xnt, u, vt, npart, e0, rank, e1, h2d, *, tt, te):
    d, m = xnt.shape
    n_exp = u.shape[0]
    rows = te // PEER_NKEYS
    row_spec = pl.BlockSpec((PEER_HEADS, rows, tt), lambda i, e: (0, e, i))
    col_spec = pl.BlockSpec((PEER_HEADS, PEER_NKEYS, tt), lambda i, e: (0, 0, i))
    tok_spec = pl.BlockSpec((tt, d), lambda i, e: (i, 0))
    return pl.pallas_call(
        functools.partial(_peer_expert_body, rows=rows),
        grid=(m // tt, n_exp // te),
        in_specs=[
            pl.BlockSpec((d, tt), lambda i, e: (0, i)),
            pl.BlockSpec((te, d), lambda i, e: (e, 0)),
            pl.BlockSpec((d, te), lambda i, e: (0, e)),
            row_spec, row_spec, col_spec, col_spec,
            tok_spec,
        ],
        out_specs=tok_spec,
        out_shape=jax.ShapeDtypeStruct((m, d), F32),
        scratch_shapes=[pltpu.VMEM((d, tt), F32), pltpu.VMEM((te, tt), BF16)],
        compiler_params=_cparams("parallel", "arbitrary"),
        name="peer_experts",
    )(xnt, u, vt, npart, e0, rank, e1, h2d)


def _transpose_cast_body(x_ref, o_ref):
    o_ref[...] = x_ref[...].T.astype(o_ref.dtype)


def transpose_cast(x, *, tr):
    r, c = x.shape
    return pl.pallas_call(
        _transpose_cast_body,
        grid=(r // tr,),
        in_specs=[pl.BlockSpec((tr, c), lambda i: (i, 0))],
        out_specs=pl.BlockSpec((c, tr), lambda i: (0, i)),
        out_shape=jax.ShapeDtypeStruct((c, r), BF16),
        compiler_params=_cparams("parallel"),
        name="transpose_cast",
    )(x)


def peer_mixer(h2d, gain, w_q, keys, u_tab, v_tab):
    qt, xnt = peer_query(h2d, gain, transpose_cast(w_q, tr=512), tm=1024, tn=512)
    npart, e0, rank, e1 = peer_route(qt, keys, tt=256)
    return peer_experts(xnt, u_tab.astype(BF16), transpose_cast(v_tab, tr=512), npart, e0, rank, e1, h2d,
                        tt=512, te=1024)


def _ple_body(x_ref, g_ref, wg_ref, p_ref, wi_ref, r_ref, o_ref, xn_ref):
    @pl.when(pl.program_id(1) == 0)
    def _():
        x = x_ref[...]
        ms = jnp.mean(x * x, axis=-1, keepdims=True)
        xn_ref[...] = (x * lax.rsqrt(ms + RMS_EPS) * g_ref[...]).astype(BF16)

    gate = jax.nn.sigmoid(jnp.dot(xn_ref[...], wg_ref[...], preferred_element_type=F32))
    emb = jnp.dot(p_ref[...], wi_ref[...], preferred_element_type=F32)
    o_ref[...] = r_ref[...] + gate * emb


def ple_mixer(h2d, gain, w_gate, p2d, w_in, *, tm, tn):
    m, d = h2d.shape
    pd = p2d.shape[1]
    return pl.pallas_call(
        _ple_body,
        grid=(m // tm, d // tn),
        in_specs=[
            pl.BlockSpec((tm, d), lambda i, j: (i, 0)),
            pl.BlockSpec((1, d), lambda i, j: (0, 0)),
            pl.BlockSpec((d, tn), lambda i, j: (0, j)),
            pl.BlockSpec((tm, pd), lambda i, j: (i, 0)),
            pl.BlockSpec((pd, tn), lambda i, j: (0, j)),
            pl.BlockSpec((tm, tn), lambda i, j: (i, j)),
        ],
        out_specs=pl.BlockSpec((tm, tn), lambda i, j: (i, j)),
        out_shape=jax.ShapeDtypeStruct((m, d), F32),
        scratch_shapes=[pltpu.VMEM((tm, d), BF16)],
        compiler_params=_cparams("parallel", "arbitrary"),
        name="ple_mixer",
    )(h2d, gain.reshape(1, d), w_gate.astype(BF16), p2d.astype(BF16), w_in.astype(BF16), h2d)


def _dn_conv_body(x_ref, halo_ref, w_ref, *rest, mode, tt, hb):
    if mode == "k":
        o_ref, ot_ref, ext_ref = rest
    else:
        o_ref, ext_ref = rest
    width, d = DN_CONV_WIDTH, DN_HEAD_DIM
    first = pl.program_id(1) == 0
    ext_ref[0:8, :] = jnp.where(first, 0.0, halo_ref[...])
    ext_ref[8:, :] = x_ref[...]
    for h in range(hb):
        lanes = slice(h * d, (h + 1) * d)
        y = None
        for j in range(width):
            term = w_ref[j:j + 1, lanes] * ext_ref[pl.ds(8 - (width - 1) + j, tt), lanes]
            y = term if y is None else y + term
        y = y * jax.nn.sigmoid(y)
        if mode in ("q", "k"):
            y = y * lax.rsqrt(jnp.sum(y * y, axis=-1, keepdims=True) + RMS_EPS)
        if mode == "q":
            y = y * (d ** -0.5)
        o_ref[0, h] = y.astype(o_ref.dtype)
        if mode == "k":
            ot_ref[0, h] = y.T.astype(ot_ref.dtype)


def dn_conv_silu(proj, conv_w, b, t, *, mode, col0, heads, tt, hb):
    nt = t // tt
    d = DN_HEAD_DIM
    cb0 = col0 // (hb * d)
    out_spec = pl.BlockSpec((1, hb, tt, d), lambda bi, ti, hi: (bi, hi, ti, 0))
    out_shape = jax.ShapeDtypeStruct((b, heads, t, d), F32)
    out_specs, out_shapes = [out_spec], [out_shape]
    if mode == "k":
        out_specs.append(pl.BlockSpec((1, hb, d, tt), lambda bi, ti, hi: (bi, hi, 0, ti)))
        out_shapes.append(jax.ShapeDtypeStruct((b, heads, d, t), F32))
    res = pl.pallas_call(
        functools.partial(_dn_conv_body, mode=mode, tt=tt, hb=hb),
        grid=(b, nt, heads // hb),
        in_specs=[
            pl.BlockSpec((tt, hb * d), lambda bi, ti, hi: (bi * nt + ti, cb0 + hi)),
            pl.BlockSpec((8, hb * d), lambda bi, ti, hi: (jnp.maximum((bi * nt + ti) * (tt // 8) - 1, 0), cb0 + hi)),
            pl.BlockSpec((DN_CONV_WIDTH, hb * d), lambda bi, ti, hi: (0, cb0 + hi)),
        ],
        out_specs=out_specs,
        out_shape=out_shapes,
        scratch_shapes=[pltpu.VMEM((tt + 8, hb * d), F32)],
        compiler_params=_cparams("parallel", "parallel", "parallel"),
        name="dn_conv_" + mode,
    )(proj, proj, conv_w)
    return res if mode == "k" else res[0]


def _dn_gate_body(ba_ref, alog_ref, dtb_ref, beta_ref, gcum_ref, *, tm):
    nh = DN_V_HEADS
    beta_ref[...] = jax.nn.sigmoid(ba_ref[0:nh, :])
    a = ba_ref[nh:2 * nh, :]
    g = -jnp.exp(alog_ref[...]) * jax.nn.softplus(a + dtb_ref[...])
    lane = lax.broadcasted_iota(I32, (nh, LANES), 1) % DN_CHUNK
    for j in range(tm // LANES):
        x = g[:, j * LANES:(j + 1) * LANES]
        s = 1
        while s < DN_CHUNK:
            x = x + jnp.where(lane >= s, pltpu.roll(x, s, 1), 0.0)
            s *= 2
        gcum_ref[:, j * LANES:(j + 1) * LANES] = x


def dn_gates(ba_t, a_log, dt_bias, *, tm):
    m = ba_t.shape[1]
    nh = DN_V_HEADS
    spec = pl.BlockSpec((nh, tm), lambda i: (0, i))
    col = pl.BlockSpec((nh, 1), lambda i: (0, 0))
    return pl.pallas_call(
        functools.partial(_dn_gate_body, tm=tm),
        grid=(m // tm,),
        in_specs=[pl.BlockSpec((ba_t.shape[0], tm), lambda i: (0, i)), col, col],
        out_specs=[spec, spec],
        out_shape=[jax.ShapeDtypeStruct((nh, m), F32)] * 2,
        compiler_params=_cparams("parallel"),
        name="dn_gates",
    )(ba_t, a_log.reshape(nh, 1), dt_bias.reshape(nh, 1))


def _dn_local_body(q_ref, k_ref, kt_ref, v_ref, gc_ref, gr_ref, bc_ref, u_ref, l1_ref, l2_ref, eg_ref, *, gsz, nchunk):
    c, d = DN_CHUNK, DN_HEAD_DIM
    rep = DN_V_HEADS // DN_QK_HEADS
    row = lax.broadcasted_iota(I32, (c, c), 0)
    colm = lax.broadcasted_iota(I32, (c, c), 1)
    tril, strict = row >= colm, row > colm
    eye = jnp.where(row == colm, 1.0, 0.0)
    bodies = [(cc, jv) for cc in range(nchunk) for jv in range(gsz)]

    kk, qk = {}, {}
    for cc in range(nchunk):
        rows = slice(cc * c, (cc + 1) * c)
        for jq in range(gsz // rep):
            kb = k_ref[0, jq, rows, :].astype(BF16)
            qb = q_ref[0, jq, rows, :].astype(BF16)
            kk[cc, jq] = lax.dot_general(kb, kb, _NT, preferred_element_type=F32)
            qk[cc, jq] = lax.dot_general(qb, kb, _NT, preferred_element_type=F32)

    x, inv, rhs = {}, {}, {}
    for cc, jv in bodies:
        rows = slice(cc * c, (cc + 1) * c)
        jq = jv // rep
        gc = gc_ref[0, 0, rows, jv:jv + 1]
        gr = gr_ref[0, 0, jv:jv + 1, rows]
        beta = bc_ref[0, 0, rows, jv:jv + 1]
        decay = jnp.exp(jnp.where(tril, gc - gr, NEG_BIG))
        a = jnp.where(strict, kk[cc, jq] * decay, 0.0) * beta
        x[cc, jv] = a
        inv[cc, jv] = eye - a
        eg = jnp.exp(gc)
        k = k_ref[0, jq, rows, :]
        rhs[cc, jv] = jnp.concatenate([v_ref[0, jv, rows, :] * beta, k * (beta * eg)], axis=1).astype(BF16)
        g_last = gc[c - 1:c, :]
        kd_t = kt_ref[0, jq, :, rows] * jnp.exp(g_last - gr)
        l2_ref[0, jv, cc] = jnp.concatenate([qk[cc, jq] * decay, kd_t], axis=0).astype(l2_ref.dtype)
        l1_ref[0, jv, cc, c:2 * c, :] = (q_ref[0, jq, rows, :] * eg).astype(l1_ref.dtype)
        eg_ref[0, jv, cc] = jnp.broadcast_to(jnp.exp(g_last), (1, d))

    span = 2
    while span < c:
        for key in bodies:
            xb = x[key].astype(BF16)
            x[key] = jnp.dot(xb, xb, preferred_element_type=F32)
        for key in bodies:
            inv[key] = inv[key] + jnp.dot(inv[key].astype(BF16), x[key].astype(BF16), preferred_element_type=F32)
        span *= 2

    for cc, jv in bodies:
        rows = slice(cc * c, (cc + 1) * c)
        sol = jnp.dot(inv[cc, jv].astype(BF16), rhs[cc, jv], preferred_element_type=F32)
        u_ref[0, jv, rows, :] = sol[:, :d]
        l1_ref[0, jv, cc, 0:c, :] = sol[:, d:].astype(l1_ref.dtype)


def dn_local(q, k, kt, v, gc_col, gc_row, beta_col, *, gsz, tb):
    b, hv, t, d = v.shape
    c = DN_CHUNK
    rep = DN_V_HEADS // DN_QK_HEADS
    nb = t // tb
    nchunk = tb // c
    qk_spec = pl.BlockSpec((1, gsz // rep, tb, d), lambda bi, gi, ni: (bi, gi, ni, 0))
    col_spec = pl.BlockSpec((1, 1, tb, gsz), lambda bi, gi, ni: (bi, gi, ni, 0))

    def chunked(rows, lanes):
        return pl.BlockSpec((1, gsz, nchunk, rows, lanes), lambda bi, gi, ni: (bi, gi, ni, 0, 0))

    return pl.pallas_call(
        functools.partial(_dn_local_body, gsz=gsz, nchunk=nchunk),
        grid=(b, hv // gsz, nb),
        in_specs=[
            qk_spec, qk_spec,
            pl.BlockSpec((1, gsz // rep, d, tb), lambda bi, gi, ni: (bi, gi, 0, ni)),
            pl.BlockSpec((1, gsz, tb, d), lambda bi, gi, ni: (bi, gi, ni, 0)),
            col_spec,
            pl.BlockSpec((1, 1, gsz, tb), lambda bi, gi, ni: (bi, gi, 0, ni)),
            col_spec,
        ],
        out_specs=[
            pl.BlockSpec((1, gsz, tb, d), lambda bi, gi, ni: (bi, gi, ni, 0)),
            chunked(2 * c, d), chunked(c + d, c), chunked(1, d),
        ],
        out_shape=[
            jax.ShapeDtypeStruct((b, hv, t, d), F32),
            jax.ShapeDtypeStruct((b, hv, t // c, 2 * c, d), BF16),
            jax.ShapeDtypeStruct((b, hv, t // c, c + d, c), BF16),
            jax.ShapeDtypeStruct((b, hv, t // c, 1, d), F32),
        ],
        compiler_params=_cparams("parallel", "parallel", "parallel"),
        name="dn_local",
    )(q, k, kt, v, gc_col, gc_row, beta_col)


def _dn_scan_body(u_ref, l1_ref, l2_ref, eg_ref, z_ref, gain_ref, o_ref, state_ref, *, gsz, nchunk):
    c, d = DN_CHUNK, DN_HEAD_DIM

    @pl.when(pl.program_id(2) == 0)
    def _():
        state_ref[...] = jnp.zeros(state_ref.shape, F32)

    gain = gain_ref[...]
    heads = range(gsz)
    for cc in range(nchunk):
        rows = slice(cc * c, (cc + 1) * c)
        r1 = [jnp.dot(l1_ref[0, jv, cc], state_ref[jv].astype(BF16), preferred_element_type=F32) for jv in heads]
        v_new = [(u_ref[0, jv, rows, :] - r1[jv][:c]).astype(BF16) for jv in heads]
        r2 = [jnp.dot(l2_ref[0, jv, cc], v_new[jv], preferred_element_type=F32) for jv in heads]
        for jv in heads:
            state_ref[jv] = state_ref[jv] * eg_ref[0, jv, cc] + r2[jv][c:]
        for jv in heads:
            o = r1[jv][c:] + r2[jv][:c]
            ms = jnp.mean(o * o, axis=-1, keepdims=True)
            z = z_ref[rows, jv * d:(jv + 1) * d]
            o_ref[rows, jv * d:(jv + 1) * d] = (
                o * lax.rsqrt(ms + RMS_EPS) * gain * (z * jax.nn.sigmoid(z))).astype(o_ref.dtype)


def dn_scan(u, l1, l2, eg, proj, z_col0, norm_gain, *, gsz, tb):
    b, hv, t, d = u.shape
    c = DN_CHUNK
    nb = t // tb
    nchunk = tb // c
    zb0 = z_col0 // (gsz * d)

    def chunked(rows, lanes):
        return pl.BlockSpec((1, gsz, nchunk, rows, lanes), lambda bi, gi, ni: (bi, gi, ni, 0, 0))

    return pl.pallas_call(
        functools.partial(_dn_scan_body, gsz=gsz, nchunk=nchunk),
        grid=(b, hv // gsz, nb),
        in_specs=[
            pl.BlockSpec((1, gsz, tb, d), lambda bi, gi, ni: (bi, gi, ni, 0)),
            chunked(2 * c, d), chunked(c + d, c), chunked(1, d),
            pl.BlockSpec((tb, gsz * d), lambda bi, gi, ni: (bi * nb + ni, zb0 + gi)),
            pl.BlockSpec((1, d), lambda bi, gi, ni: (0, 0)),
        ],
        out_specs=pl.BlockSpec((tb, gsz * d), lambda bi, gi, ni: (bi * nb + ni, gi)),
        out_shape=jax.ShapeDtypeStruct((b * t, hv * d), BF16),
        scratch_shapes=[pltpu.VMEM((gsz, d, d), F32)],
        compiler_params=_cparams("parallel", "parallel", "arbitrary"),
        name="dn_scan",
    )(u, l1, l2, eg, proj, norm_gain.reshape(1, d))


def gdn_mixer(h2d, b, t, norm_gain, w_in, conv_w, a_log, dt_bias, out_gain, w_out):
    key_dim = DN_QK_HEADS * DN_HEAD_DIM
    val_dim = DN_V_HEADS * DN_HEAD_DIM
    main = 2 * key_dim + 2 * val_dim
    proj = norm_matmul(h2d, norm_gain, w_in[:, :main].astype(BF16), tm=1024, tn=1024)
    w_ba_t = _pad_cols(w_in[:, main:], LANES).T.astype(BF16)
    ba_t, _ = peer_query(h2d, norm_gain, w_ba_t, tm=512, tn=LANES)
    beta_t, gcum_t = dn_gates(ba_t, a_log, dt_bias, tm=512)
    q = dn_conv_silu(proj, conv_w, b, t, mode="q", col0=0, heads=DN_QK_HEADS, tt=512, hb=8)
    k, kt = dn_conv_silu(proj, conv_w, b, t, mode="k", col0=key_dim, heads=DN_QK_HEADS, tt=512, hb=8)
    v = dn_conv_silu(proj, conv_w, b, t, mode="v", col0=2 * key_dim, heads=DN_V_HEADS, tt=512, hb=8)
    gsz = 8
    hg = DN_V_HEADS // gsz
    rows = lambda x: x.reshape(hg, gsz, b, t).transpose(2, 0, 1, 3)
    cols = lambda x: rows(x).transpose(0, 1, 3, 2)
    u, l1, l2, eg = dn_local(q, k, kt, v, cols(gcum_t), rows(gcum_t), cols(beta_t), gsz=gsz, tb=4 * DN_CHUNK)
    o = dn_scan(u, l1, l2, eg, proj, 2 * key_dim + val_dim, out_gain, gsz=2 * gsz, tb=4 * DN_CHUNK)
    return matmul_residual(o, w_out.astype(BF16), h2d, tm=1024, tn=512)


def kernel(x, p, norm_mix, norm_ffn, norm_ple, attn_w_in, attn_q_norm, attn_k_norm, attn_w_out, dn_w_in, dn_conv,
           dn_a_log, dn_dt_bias, dn_norm, dn_w_out, peer_w_q, peer_keys, peer_u, peer_v, ple_w_in, ple_w_gate):
    b, t, d = x.shape
    n_mixers = 2
    h = x.reshape(b * t, d)
    for i in range(p.shape[0]):
        j = i // n_mixers
        if i % n_mixers == 0:
            h = dsa_mixer(h, b, t, norm_mix[i], attn_w_in[j], attn_q_norm[j], attn_k_norm[j], attn_w_out[j])
        else:
            h = gdn_mixer(h, b, t, norm_mix[i], dn_w_in[j], dn_conv[j], dn_a_log[j], dn_dt_bias[j], dn_norm[j],
                          dn_w_out[j])
        h = peer_mixer(h, norm_ffn[i], peer_w_q[i], peer_keys[i], peer_u[i], peer_v[i])
        h = ple_mixer(h, norm_ple[i], ple_w_gate[i], p[i].reshape(b * t, -1), ple_w_in[i], tm=1024, tn=512)
    return h.reshape(b, t, d)
```

```python
import functools

import jax
import jax.numpy as jnp
import numpy as np
from jax import lax
from jax.experimental import pallas as pl
from jax.experimental.pallas import tpu as pltpu

F32 = jnp.float32
BF16 = jnp.bfloat16
I32 = jnp.int32
I16 = jnp.int16
HALF16 = 2 ** 15

RMS_EPS = 1e-6
ROPE_THETA = 500000.0
ROT_FRACTION = 4

N_HEADS = 16
N_KV_HEADS = 4
HEAD_DIM = 128
IDX_HEADS = 16
IDX_DIM = 64
INDEX_TOPK = 256
V_ROWS = HEAD_DIM + 16

DN_QK_HEADS = 16
DN_V_HEADS = 32
DN_HEAD_DIM = 128
DN_CONV_WIDTH = 4
DN_CHUNK = 64

PEER_HEADS = 8
PEER_NKEYS = 128
PEER_TOPK = 16

LANES = 128
VMEM_LIMIT = 56 * 1024 * 1024

LOG2E = 1.4426950408889634
INT_MIN = -(2 ** 31)
NEG_BIG = -1e30

_NT = (((1,), (1,)), ((), ()))


def _cparams(*sem):
    return pltpu.CompilerParams(dimension_semantics=sem, vmem_limit_bytes=VMEM_LIMIT)


def _norm_mm_body(x_ref, g_ref, w_ref, o_ref, xn_ref):
    @pl.when(pl.program_id(1) == 0)
    def _():
        x = x_ref[...]
        ms = jnp.mean(x * x, axis=-1, keepdims=True)
        xn_ref[...] = (x * lax.rsqrt(ms + RMS_EPS) * g_ref[...]).astype(BF16)

    o_ref[...] = jnp.dot(xn_ref[...], w_ref[...], preferred_element_type=F32).astype(o_ref.dtype)


def norm_matmul(x, gain, w, *, tm, tn, out_dtype=F32):
    m, k = x.shape
    n = w.shape[1]
    return pl.pallas_call(
        _norm_mm_body,
        grid=(m // tm, n // tn),
        in_specs=[
            pl.BlockSpec((tm, k), lambda i, j: (i, 0)),
            pl.BlockSpec((1, k), lambda i, j: (0, 0)),
            pl.BlockSpec((k, tn), lambda i, j: (0, j)),
        ],
        out_specs=pl.BlockSpec((tm, tn), lambda i, j: (i, j)),
        out_shape=jax.ShapeDtypeStruct((m, n), out_dtype),
        scratch_shapes=[pltpu.VMEM((tm, k), BF16)],
        compiler_params=_cparams("parallel", "arbitrary"),
        name="norm_matmul",
    )(x, gain.reshape(1, k), w)


def _mm_res_body(x_ref, w_ref, r_ref, o_ref):
    o_ref[...] = r_ref[...] + jnp.dot(x_ref[...], w_ref[...], preferred_element_type=F32)


def matmul_residual(x, w, res, *, tm, tn):
    m, k = x.shape
    n = w.shape[1]
    return pl.pallas_call(
        _mm_res_body,
        grid=(m // tm, n // tn),
        in_specs=[
            pl.BlockSpec((tm, k), lambda i, j: (i, 0)),
            pl.BlockSpec((k, tn), lambda i, j: (0, j)),
            pl.BlockSpec((tm, tn), lambda i, j: (i, j)),
        ],
        out_specs=pl.BlockSpec((tm, tn), lambda i, j: (i, j)),
        out_shape=jax.ShapeDtypeStruct((m, n), F32),
        compiler_params=_cparams("parallel", "arbitrary"),
        name="matmul_residual",
    )(x, w, res)


def _rope_tables(t, head_dim):
    rot = head_dim // ROT_FRACTION
    half = rot // 2
    inv_freq = ROPE_THETA ** (-jnp.arange(half, dtype=F32) * (2.0 / rot))
    ang = jnp.arange(t, dtype=jnp.int32).astype(F32)[:, None] * inv_freq[None, :]
    cos, sin = jnp.cos(ang), jnp.sin(ang)
    rest = head_dim - rot
    c = jnp.concatenate([cos, cos, jnp.ones((t, rest), F32)], axis=-1)
    s_lo = jnp.concatenate([-sin, jnp.zeros((t, half + rest), F32)], axis=-1)
    s_hi = jnp.concatenate([jnp.zeros((t, half), F32), sin, jnp.zeros((t, rest), F32)], axis=-1)
    reps = LANES // head_dim
    return tuple(jnp.tile(a, (1, reps)) for a in (c, s_lo, s_hi)), half


def _rope_tile(x, c, s_lo, s_hi, half):
    return x * c + pltpu.roll(x, LANES - half, 1) * s_lo + pltpu.roll(x, half, 1) * s_hi


def _attn_prep_body(p_ref, qg_ref, kg_ref, c128_ref, sl128_ref, sh128_ref, c64_ref, sl64_ref, sh64_ref,
                    q_ref, k_ref, vt_ref, iq_ref, ik_ref, *, half128, half64, qb):
    c128, sl128, sh128 = c128_ref[...], sl128_ref[...], sh128_ref[...]
    c64, sl64, sh64 = c64_ref[...], sl64_ref[...], sh64_ref[...]

    def normed(x, g):
        ms = jnp.mean(x * x, axis=-1, keepdims=True)
        return x * lax.rsqrt(ms + RMS_EPS) * g

    off = 0
    nblk = p_ref.shape[0] // qb
    group = N_HEADS // N_KV_HEADS
    for h in range(N_HEADS):
        x = normed(p_ref[:, off:off + HEAD_DIM], qg_ref[...])
        xt = (_rope_tile(x, c128, sl128, sh128, half128) * (HEAD_DIM ** -0.5 * LOG2E)).T.astype(BF16)
        n, g = divmod(h, group)
        for j in range(nblk):
            c0 = (j * group + g) * qb
            q_ref[0, n, :, c0:c0 + qb] = xt[:, j * qb:(j + 1) * qb]
        off += HEAD_DIM
    for h in range(N_KV_HEADS):
        x = normed(p_ref[:, off:off + HEAD_DIM], kg_ref[...])
        k_ref[0, h] = _rope_tile(x, c128, sl128, sh128, half128).astype(BF16)
        off += HEAD_DIM
    ones_rows = jnp.ones((V_ROWS - HEAD_DIM, p_ref.shape[0]), BF16)
    for h in range(N_KV_HEADS):
        vt_ref[0, h, 0:HEAD_DIM, :] = p_ref[:, off:off + HEAD_DIM].T.astype(BF16)
        vt_ref[0, h, HEAD_DIM:V_ROWS, :] = ones_rows
        off += HEAD_DIM
    for j in range(IDX_HEADS * IDX_DIM // LANES):
        xt = (_rope_tile(p_ref[:, off:off + LANES], c64, sl64, sh64, half64) * (IDX_DIM ** -0.5)).T.astype(BF16)
        for u in range(LANES // IDX_DIM):
            h = j * (LANES // IDX_DIM) + u
            for jb in range(nblk):
                c0 = (jb * IDX_HEADS + h) * qb
                iq_ref[0, :, c0:c0 + qb] = xt[u * IDX_DIM:(u + 1) * IDX_DIM, jb * qb:(jb + 1) * qb]
        off += LANES
    x = _rope_tile(p_ref[:, off:off + LANES], c64, sl64, sh64, half64)
    ik_ref[0] = x[:, :IDX_DIM].astype(BF16)


def attn_prep(proj, q_gain, k_gain, b, t, *, tt, qb):
    (c128, sl128, sh128), half128 = _rope_tables(t, HEAD_DIM)
    (c64, sl64, sh64), half64 = _rope_tables(t, IDX_DIM)
    nt = t // tt
    width = proj.shape[1]
    group = N_HEADS // N_KV_HEADS
    tab = pl.BlockSpec((tt, LANES), lambda bi, ti: (ti, 0))
    gain = pl.BlockSpec((1, HEAD_DIM), lambda bi, ti: (0, 0))
    return pl.pallas_call(
        functools.partial(_attn_prep_body, half128=half128, half64=half64, qb=qb),
        grid=(b, nt),
        in_specs=[pl.BlockSpec((tt, width), lambda bi, ti: (bi * nt + ti, 0)), gain, gain,
                  tab, tab, tab, tab, tab, tab],
        out_specs=[
            pl.BlockSpec((1, N_KV_HEADS, HEAD_DIM, group * tt), lambda bi, ti: (bi, 0, 0, ti)),
            pl.BlockSpec((1, N_KV_HEADS, tt, HEAD_DIM), lambda bi, ti: (bi, 0, ti, 0)),
            pl.BlockSpec((1, N_KV_HEADS, V_ROWS, tt), lambda bi, ti: (bi, 0, 0, ti)),
            pl.BlockSpec((1, IDX_DIM, IDX_HEADS * tt), lambda bi, ti: (bi, 0, ti)),
            pl.BlockSpec((1, tt, IDX_DIM), lambda bi, ti: (bi, ti, 0)),
        ],
        out_shape=[
            jax.ShapeDtypeStruct((b, N_KV_HEADS, HEAD_DIM, group * t), BF16),
            jax.ShapeDtypeStruct((b, N_KV_HEADS, t, HEAD_DIM), BF16),
            jax.ShapeDtypeStruct((b, N_KV_HEADS, V_ROWS, t), BF16),
            jax.ShapeDtypeStruct((b, IDX_DIM, IDX_HEADS * t), BF16),
            jax.ShapeDtypeStruct((b, t, IDX_DIM), BF16),
        ],
        compiler_params=_cparams("parallel", "parallel"),
        name="attn_prep",
    )(proj, q_gain.reshape(1, HEAD_DIM), k_gain.reshape(1, HEAD_DIM), c128, sl128, sh128, c64, sl64, sh64)


def _attn_body(iq_ref, iw_ref, ik_ref, q_ref, k_ref, vt_ref, o_ref,
               keys_ref, hi_ref, lo_ref, m_ref, acc_ref, *, qb, kc, topk):
    qi = pl.program_id(1)
    group = N_HEADS // N_KV_HEADS
    nck = ((qi + 1) * qb + kc - 1) // kc
    iq = iq_ref[0]
    iw = iw_ref[0, 0] * (IDX_HEADS ** -0.5)
    qpos = qi * qb + lax.broadcasted_iota(I32, (kc, qb), 1)
    krow = lax.broadcasted_iota(I32, (kc, qb), 0)

    def score_chunk(c, carry):
        off = pl.multiple_of(c * kc, kc)
        ikc = ik_ref[0, pl.ds(off, kc), :]
        s = jnp.dot(ikc, iq, preferred_element_type=F32)
        s = jnp.maximum(s, 0.0) * iw
        sc = s[:, 0:qb]
        for h in range(1, IDX_HEADS):
            sc = sc + s[:, h * qb:(h + 1) * qb]
        bits = pltpu.bitcast(sc, I32)
        key = bits ^ ((bits >> 31) & 0x7FFFFFFF)
        key = jnp.where(krow + off <= qpos, key, INT_MIN)
        keys_ref[c] = key
        half = pl.ds(pl.multiple_of((c % 2) * kc, kc), kc)
        hi_ref[c // 2, half, :] = (key >> 16).astype(I16)
        lo_ref[c // 2, half, :] = ((key & 0xFFFF) - HALF16).astype(I16)
        return carry

    lax.fori_loop(0, nck, score_chunk, 0)

    lowest16 = jnp.full((kc, qb), -HALF16, I16)

    @pl.when(nck % 2 == 1)
    def _():
        hi_ref[nck // 2, kc:2 * kc, :] = lowest16
        lo_ref[nck // 2, kc:2 * kc, :] = lowest16

    npair = (nck + 1) // 2
    one16, zero16 = jnp.ones((), I16), jnp.zeros((), I16)

    def count_ge(ref, cand):
        cand16 = jnp.broadcast_to(cand.astype(I16), (16, qb))

        def body(c, acc):
            parts = [jnp.where(ref[c, r * 16:(r + 1) * 16, :] >= cand16, one16, zero16)
                     for r in range(2 * kc // 16)]
            while len(parts) > 1:
                parts = [parts[i] + parts[i + 1] for i in range(0, len(parts), 2)]
            return acc + parts[0].astype(I32)

        acc = lax.fori_loop(0, npair, body, jnp.zeros((16, qb), I32))
        return acc.sum(axis=0, keepdims=True)

    def largest_with_count(ref, need):
        zero = jnp.zeros((1, qb), I32)
        ans = jnp.where(count_ge(ref, zero) >= need, zero, -HALF16)

        def bit_step(i, ans):
            cand = ans | lax.shift_left(jnp.int32(1), 14 - i)
            return jnp.where(count_ge(ref, cand) >= need, cand, ans)

        return lax.fori_loop(0, 15, bit_step, ans)

    ans_hi = largest_with_count(hi_ref, topk)
    above = jnp.where(ans_hi == HALF16 - 1, 0, count_ge(hi_ref, jnp.minimum(ans_hi + 1, HALF16 - 1)))
    hi16 = jnp.broadcast_to(ans_hi.astype(I16), (2 * kc, qb))

    def mask_low(c, carry):
        lo_ref[c] = jnp.where(hi_ref[c] == hi16, lo_ref[c], jnp.full((), -HALF16, I16))
        return carry

    lax.fori_loop(0, npair, mask_low, 0)
    ans_lo = largest_with_count(lo_ref, topk - above)
    tau = lax.shift_left(ans_hi, 16) | (ans_lo + HALF16)
    tau = jnp.maximum(tau, INT_MIN + 1)

    m_ref[...] = jnp.full(m_ref.shape, NEG_BIG, F32)
    acc_ref[...] = jnp.zeros(acc_ref.shape, F32)

    def attn_chunk(c, carry):
        off = pl.multiple_of(c * kc, kc)
        bias = jnp.where(keys_ref[c] >= tau, 0.0, NEG_BIG)
        bias = jnp.concatenate([bias] * group, axis=1)
        heads = range(N_KV_HEADS)
        s = [jnp.dot(k_ref[0, n, pl.ds(off, kc), :], q_ref[0, n], preferred_element_type=F32) + bias
             for n in heads]
        m_new = [jnp.maximum(m_ref[n], s[n].max(axis=0, keepdims=True)) for n in heads]
        p = [jnp.exp2(s[n] - m_new[n]) for n in heads]
        pv = [jnp.dot(vt_ref[0, n, :, pl.ds(off, kc)], p[n].astype(BF16), preferred_element_type=F32)
              for n in heads]
        for n in heads:
            acc_ref[n] = jnp.exp2(m_ref[n] - m_new[n]) * acc_ref[n] + pv[n]
            m_ref[n] = m_new[n]
        return carry

    lax.fori_loop(0, nck, attn_chunk, 0)

    for n in range(N_KV_HEADS):
        on = acc_ref[n, 0:HEAD_DIM, :] / acc_ref[n, HEAD_DIM:HEAD_DIM + 1, :]
        for g in range(group):
            h = n * group + g
            o_ref[0, :, h * HEAD_DIM:(h + 1) * HEAD_DIM] = on[:, g * qb:(g + 1) * qb].T.astype(o_ref.dtype)


def sparse_attention(q, k, vt, iq, ik, iw, *, qb, kc):
    b, _, t, _ = k.shape
    nq = t // qb
    topk = min(INDEX_TOPK, t // 4)
    group = N_HEADS // N_KV_HEADS
    iw_rows = iw.reshape(b, nq, qb, IDX_HEADS).transpose(0, 1, 3, 2).reshape(b, nq, 1, IDX_HEADS * qb)
    return pl.pallas_call(
        functools.partial(_attn_body, qb=qb, kc=kc, topk=topk),
        grid=(b, nq),
        in_specs=[
            pl.BlockSpec((1, IDX_DIM, IDX_HEADS * qb), lambda bi, qi: (bi, 0, qi)),
            pl.BlockSpec((1, 1, 1, IDX_HEADS * qb), lambda bi, qi: (bi, qi, 0, 0)),
            pl.BlockSpec((1, t, IDX_DIM), lambda bi, qi: (bi, 0, 0)),
            pl.BlockSpec((1, N_KV_HEADS, HEAD_DIM, group * qb), lambda bi, qi: (bi, 0, 0, qi)),
            pl.BlockSpec((1, N_KV_HEADS, t, HEAD_DIM), lambda bi, qi: (bi, 0, 0, 0)),
            pl.BlockSpec((1, N_KV_HEADS, V_ROWS, t), lambda bi, qi: (bi, 0, 0, 0)),
        ],
        out_specs=pl.BlockSpec((1, qb, N_HEADS * HEAD_DIM), lambda bi, qi: (bi, qi, 0)),
        out_shape=jax.ShapeDtypeStruct((b, t, N_HEADS * HEAD_DIM), BF16),
        scratch_shapes=[
            pltpu.VMEM((t // kc, kc, qb), I32),
            pltpu.VMEM((t // (2 * kc), 2 * kc, qb), I16),
            pltpu.VMEM((t // (2 * kc), 2 * kc, qb), I16),
            pltpu.VMEM((N_KV_HEADS, 1, group * qb), F32),
            pltpu.VMEM((N_KV_HEADS, V_ROWS, group * qb), F32),
        ],
        compiler_params=_cparams("parallel", "arbitrary"),
        name="sparse_attention",
    )(iq, iw_rows, ik, q, k, vt)


def _pad_cols(w, mult):
    n = w.shape[1]
    pad = (-n) % mult
    return jnp.pad(w, ((0, 0), (0, pad))) if pad else w


def dsa_mixer(h2d, b, t, norm_gain, w_in, q_gain, k_gain, w_out):
    w = _pad_cols(w_in, LANES).astype(BF16)
    proj = norm_matmul(h2d, norm_gain, w, tm=1024, tn=w.shape[1] // 3)
    q, k, vt, iq, ik = attn_prep(proj, q_gain, k_gain, b, t, tt=256, qb=128)
    iw_off = N_HEADS * HEAD_DIM + 2 * N_KV_HEADS * HEAD_DIM + IDX_HEADS * IDX_DIM + IDX_DIM
    iw = proj[:, iw_off:iw_off + IDX_HEADS].reshape(b, t, IDX_HEADS)
    o = sparse_attention(q, k, vt, iq, ik, iw, qb=128, kc=512)
    return matmul_residual(o.reshape(b * t, -1), w_out.astype(BF16), h2d, tm=1024, tn=512)


def _peer_q_body(x_ref, g_ref, w_ref, qt_ref, xnt_ref):
    @pl.when(pl.program_id(1) == 0)
    def _():
        x = x_ref[...]
        ms = jnp.mean(x * x, axis=-1, keepdims=True)
        xnt_ref[...] = (x * lax.rsqrt(ms + RMS_EPS) * g_ref[...]).T.astype(BF16)

    qt_ref[...] = jnp.dot(w_ref[...], xnt_ref[...], preferred_element_type=F32)


def peer_query(h2d, gain, w_qt, *, tm, tn):
    m, k = h2d.shape
    n = w_qt.shape[0]
    return pl.pallas_call(
        _peer_q_body,
        grid=(m // tm, n // tn),
        in_specs=[
            pl.BlockSpec((tm, k), lambda i, j: (i, 0)),
            pl.BlockSpec((1, k), lambda i, j: (0, 0)),
            pl.BlockSpec((tn, k), lambda i, j: (j, 0)),
        ],
        out_specs=[
            pl.BlockSpec((tn, tm), lambda i, j: (j, i)),
            pl.BlockSpec((k, tm), lambda i, j: (0, i)),
        ],
        out_shape=[jax.ShapeDtypeStruct((n, m), F32), jax.ShapeDtypeStruct((k, m), BF16)],
        compiler_params=_cparams("parallel", "arbitrary"),
        name="peer_query",
    )(h2d, gain.reshape(1, k), w_qt)


def _top_values(x, dst_ref, n):
    rank = jnp.full(x.shape, float(n + 1), F32)
    for r in range(n):
        m = x.max(axis=0, keepdims=True)
        dst_ref[r:r + 1, :] = m
        hit = x == m
        rank = jnp.where(hit, float(r + 1), rank)
        x = jnp.where(hit, -jnp.inf, x)
    return rank


def _peer_route_body(qt_ref, keys_ref, n_ref, e0_ref, rank_ref, e1_ref, a_ref, b_ref, cand_ref, f_ref):
    k = PEER_TOPK
    for h in range(PEER_HEADS):
        s, ranks = [], []
        for p, dst in ((0, a_ref), (1, b_ref)):
            r0 = (h * 2 + p) * PEER_NKEYS
            sp = jnp.dot(keys_ref[h, p], qt_ref[r0:r0 + PEER_NKEYS, :],
                         precision=lax.Precision.HIGHEST, preferred_element_type=F32)
            s.append(sp)
            ranks.append(_top_values(sp, dst, k))
        spans, off = [], 0
        for i in range(k):
            spans.append((off, k // (i + 1)))
            off += k // (i + 1)
        cand_ref[off - off % 8:, :] = jnp.full((cand_ref.shape[0] - off + off % 8, cand_ref.shape[1]), -jnp.inf, F32)
        for i, (o, n) in enumerate(spans):
            cand_ref[o:o + n, :] = a_ref[i:i + 1, :] + b_ref[0:n, :]
        _top_values(cand_ref[...], f_ref, k)
        f = f_ref[...]
        z = jnp.exp(f - f[0:1, :]).sum(axis=0, keepdims=True)
        tau = f[k - 1:k, :]
        partners = jnp.zeros(s[0].shape, F32)
        for i, (o, n) in enumerate(spans):
            cnt = jnp.where(cand_ref[o:o + n, :] >= tau, 1.0, 0.0).sum(axis=0, keepdims=True)
            partners = jnp.where(ranks[0] == float(i + 1), cnt, partners)
        n_ref[h] = partners
        e0_ref[h] = jnp.exp(s[0] - a_ref[0:1, :])
        rank_ref[h] = ranks[1].astype(rank_ref.dtype)
        e1_ref[h] = (jnp.exp(s[1] - b_ref[0:1, :]) / z).astype(e1_ref.dtype)


def peer_route(qt, keys, *, tt):
    n, m = qt.shape
    spec = pl.BlockSpec((PEER_HEADS, PEER_NKEYS, tt), lambda i: (0, 0, i))
    shp = jax.ShapeDtypeStruct((PEER_HEADS, PEER_NKEYS, m), F32)
    shp16 = jax.ShapeDtypeStruct((PEER_HEADS, PEER_NKEYS, m), BF16)
    n_cand = sum(PEER_TOPK // (i + 1) for i in range(PEER_TOPK))
    return pl.pallas_call(
        _peer_route_body,
        grid=(m // tt,),
        in_specs=[
            pl.BlockSpec((n, tt), lambda i: (0, i)),
            pl.BlockSpec(keys.shape, lambda i: (0, 0, 0, 0)),
        ],
        out_specs=[spec, spec, spec, spec],
        out_shape=[shp, shp, shp16, shp16],
        scratch_shapes=[
            pltpu.VMEM((PEER_TOPK, tt), F32),
            pltpu.VMEM((PEER_TOPK, tt), F32),
            pltpu.VMEM((-(-n_cand // 8) * 8, tt), F32),
            pltpu.VMEM((PEER_TOPK, tt), F32),
        ],
        compiler_params=_cparams("parallel"),
        name="peer_route",
    )(qt, keys)


def _gelu(x):
    return 0.5 * x * (1.0 + lax.erf(x * (2.0 ** -0.5)))


def _peer_expert_body(xnt_ref, u_ref, vt_ref, n_ref, e0_ref, rank_ref, e1_ref, h_ref, o_ref, acc_ref, coef_ref, *, rows):
    e = pl.program_id(1)

    @pl.when(e == 0)
    def _():
        acc_ref[...] = jnp.zeros(acc_ref.shape, F32)

    act = jnp.dot(u_ref[...], xnt_ref[...], preferred_element_type=F32)
    for ii in range(rows):
        w = None
        for hd in range(PEER_HEADS):
            npart = n_ref[hd, ii:ii + 1, :].astype(BF16)
            e0 = e0_ref[hd, ii:ii + 1, :].astype(BF16)
            t = jnp.where(rank_ref[hd] <= npart, e1_ref[hd], 0.0) * e0
            w = t if w is None else w + t
        sl = slice(ii * PEER_NKEYS, (ii + 1) * PEER_NKEYS)
        coef_ref[sl, :] = w * _gelu(act[sl, :]).astype(BF16)
    acc_ref[...] += jnp.dot(vt_ref[...], coef_ref[...], preferred_element_type=F32)

    @pl.when(e == pl.num_programs(1) - 1)
    def _():
        o_ref[...] = h_ref[...] + acc_ref[...].T


def peer_experts(xnt, u, vt, npart, e0, rank, e1, h2d, *, tt, te):
    d, m = xnt.shape
    n_exp = u.shape[0]
    rows = te // PEER_NKEYS
    row_spec = pl.BlockSpec((PEER_HEADS, rows, tt), lambda i, e: (0, e, i))
    col_spec = pl.BlockSpec((PEER_HEADS, PEER_NKEYS, tt), lambda i, e: (0, 0, i))
    tok_spec = pl.BlockSpec((tt, d), lambda i, e: (i, 0))
    return pl.pallas_call(
        functools.partial(_peer_expert_body, rows=rows),
        grid=(m // tt, n_exp // te),
        in_specs=[
            pl.BlockSpec((d, tt), lambda i, e: (0, i)),
            pl.BlockSpec((te, d), lambda i, e: (e, 0)),
            pl.BlockSpec((d, te), lambda i, e: (0, e)),
            row_spec, row_spec, col_spec, col_spec,
            tok_spec,
        ],
        out_specs=tok_spec,
        out_shape=jax.ShapeDtypeStruct((m, d), F32),
        scratch_shapes=[pltpu.VMEM((d, tt), F32), pltpu.VMEM((te, tt), BF16)],
        compiler_params=_cparams("parallel", "arbitrary"),
        name="peer_experts",
    )(xnt, u, vt, npart, e0, rank, e1, h2d)


def _layer_cast_body(x_ref, o_ref, *, transpose):
    x = x_ref[0]
    o_ref[...] = (x.T if transpose else x).astype(o_ref.dtype)


def layer_cast(x, layer, *, tr, transpose):
    _, r, c = x.shape
    return pl.pallas_call(
        functools.partial(_layer_cast_body, transpose=transpose),
        grid=(r // tr,),
        in_specs=[pl.BlockSpec((1, tr, c), lambda i: (layer, i, 0))],
        out_specs=pl.BlockSpec((c, tr), lambda i: (0, i)) if transpose else pl.BlockSpec((tr, c), lambda i: (i, 0)),
        out_shape=jax.ShapeDtypeStruct((c, r) if transpose else (r, c), BF16),
        compiler_params=_cparams("parallel"),
        name="layer_cast_t" if transpose else "layer_cast",
    )(x)


def peer_mixer(h2d, gain, layer, w_q, keys, u_tab, v_tab):
    qt, xnt = peer_query(h2d, gain, layer_cast(w_q, layer, tr=512, transpose=True), tm=1024, tn=512)
    npart, e0, rank, e1 = peer_route(qt, keys, tt=256)
    u16 = layer_cast(u_tab, layer, tr=512, transpose=False)
    vt16 = layer_cast(v_tab, layer, tr=512, transpose=True)
    return peer_experts(xnt, u16, vt16, npart, e0, rank, e1, h2d, tt=512, te=1024)


def _ple_body(x_ref, g_ref, wg_ref, p_ref, wi_ref, r_ref, o_ref, xn_ref):
    @pl.when(pl.program_id(1) == 0)
    def _():
        x = x_ref[...]
        ms = jnp.mean(x * x, axis=-1, keepdims=True)
        xn_ref[...] = (x * lax.rsqrt(ms + RMS_EPS) * g_ref[...]).astype(BF16)

    gate = jax.nn.sigmoid(jnp.dot(xn_ref[...], wg_ref[...], preferred_element_type=F32))
    emb = jnp.dot(p_ref[...], wi_ref[...], preferred_element_type=F32)
    o_ref[...] = r_ref[...] + gate * emb


def ple_mixer(h2d, gain, w_gate, p2d, w_in, *, tm, tn):
    m, d = h2d.shape
    pd = p2d.shape[1]
    return pl.pallas_call(
        _ple_body,
        grid=(m // tm, d // tn),
        in_specs=[
            pl.BlockSpec((tm, d), lambda i, j: (i, 0)),
            pl.BlockSpec((1, d), lambda i, j: (0, 0)),
            pl.BlockSpec((d, tn), lambda i, j: (0, j)),
            pl.BlockSpec((tm, pd), lambda i, j: (i, 0)),
            pl.BlockSpec((pd, tn), lambda i, j: (0, j)),
            pl.BlockSpec((tm, tn), lambda i, j: (i, j)),
        ],
        out_specs=pl.BlockSpec((tm, tn), lambda i, j: (i, j)),
        out_shape=jax.ShapeDtypeStruct((m, d), F32),
        scratch_shapes=[pltpu.VMEM((tm, d), BF16)],
        compiler_params=_cparams("parallel", "arbitrary"),
        name="ple_mixer",
    )(h2d, gain.reshape(1, d), w_gate.astype(BF16), p2d.astype(BF16), w_in.astype(BF16), h2d)


def _dn_conv_body(x_ref, halo_ref, w_ref, *rest, mode, tt, hb):
    if mode == "k":
        o_ref, ot_ref, ext_ref = rest
    else:
        o_ref, ext_ref = rest
    width, d = DN_CONV_WIDTH, DN_HEAD_DIM
    first = pl.program_id(1) == 0
    ext_ref[0:8, :] = jnp.where(first, 0.0, halo_ref[...])
    ext_ref[8:, :] = x_ref[...]
    for h in range(hb):
        lanes = slice(h * d, (h + 1) * d)
        y = None
        for j in range(width):
            term = w_ref[j:j + 1, lanes] * ext_ref[pl.ds(8 - (width - 1) + j, tt), lanes]
            y = term if y is None else y + term
        y = y * jax.nn.sigmoid(y)
        if mode in ("q", "k"):
            y = y * lax.rsqrt(jnp.sum(y * y, axis=-1, keepdims=True) + RMS_EPS)
        if mode == "q":
            y = y * (d ** -0.5)
        o_ref[0, h] = y.astype(o_ref.dtype)
        if mode == "k":
            ot_ref[0, h] = y.T.astype(ot_ref.dtype)


def dn_conv_silu(proj, conv_w, b, t, *, mode, col0, heads, tt, hb):
    nt = t // tt
    d = DN_HEAD_DIM
    cb0 = col0 // (hb * d)
    out_spec = pl.BlockSpec((1, hb, tt, d), lambda bi, ti, hi: (bi, hi, ti, 0))
    out_shape = jax.ShapeDtypeStruct((b, heads, t, d), F32)
    out_specs, out_shapes = [out_spec], [out_shape]
    if mode == "k":
        out_specs.append(pl.BlockSpec((1, hb, d, tt), lambda bi, ti, hi: (bi, hi, 0, ti)))
        out_shapes.append(jax.ShapeDtypeStruct((b, heads, d, t), F32))
    res = pl.pallas_call(
        functools.partial(_dn_conv_body, mode=mode, tt=tt, hb=hb),
        grid=(b, nt, heads // hb),
        in_specs=[
            pl.BlockSpec((tt, hb * d), lambda bi, ti, hi: (bi * nt + ti, cb0 + hi)),
            pl.BlockSpec((8, hb * d), lambda bi, ti, hi: (jnp.maximum((bi * nt + ti) * (tt // 8) - 1, 0), cb0 + hi)),
            pl.BlockSpec((DN_CONV_WIDTH, hb * d), lambda bi, ti, hi: (0, cb0 + hi)),
        ],
        out_specs=out_specs,
        out_shape=out_shapes,
        scratch_shapes=[pltpu.VMEM((tt + 8, hb * d), F32)],
        compiler_params=_cparams("parallel", "parallel", "parallel"),
        name="dn_conv_" + mode,
    )(proj, proj, conv_w)
    return res if mode == "k" else res[0]


def _dn_gate_body(ba_ref, alog_ref, dtb_ref, beta_ref, gcum_ref, *, tm):
    nh = DN_V_HEADS
    beta_ref[...] = jax.nn.sigmoid(ba_ref[0:nh, :])
    a = ba_ref[nh:2 * nh, :]
    g = -jnp.exp(alog_ref[...]) * jax.nn.softplus(a + dtb_ref[...])
    lane = lax.broadcasted_iota(I32, (nh, LANES), 1) % DN_CHUNK
    for j in range(tm // LANES):
        x = g[:, j * LANES:(j + 1) * LANES]
        s = 1
        while s < DN_CHUNK:
            x = x + jnp.where(lane >= s, pltpu.roll(x, s, 1), 0.0)
            s *= 2
        gcum_ref[:, j * LANES:(j + 1) * LANES] = x


def dn_gates(ba_t, a_log, dt_bias, *, tm):
    m = ba_t.shape[1]
    nh = DN_V_HEADS
    spec = pl.BlockSpec((nh, tm), lambda i: (0, i))
    col = pl.BlockSpec((nh, 1), lambda i: (0, 0))
    return pl.pallas_call(
        functools.partial(_dn_gate_body, tm=tm),
        grid=(m // tm,),
        in_specs=[pl.BlockSpec((ba_t.shape[0], tm), lambda i: (0, i)), col, col],
        out_specs=[spec, spec],
        out_shape=[jax.ShapeDtypeStruct((nh, m), F32)] * 2,
        compiler_params=_cparams("parallel"),
        name="dn_gates",
    )(ba_t, a_log.reshape(nh, 1), dt_bias.reshape(nh, 1))


def _dn_local_body(q_ref, k_ref, kt_ref, v_ref, gc_ref, gr_ref, bc_ref, u_ref, l1_ref, l2_ref, eg_ref, *, gsz, nchunk):
    c, d = DN_CHUNK, DN_HEAD_DIM
    rep = DN_V_HEADS // DN_QK_HEADS
    row = lax.broadcasted_iota(I32, (c, c), 0)
    colm = lax.broadcasted_iota(I32, (c, c), 1)
    tril, strict = row >= colm, row > colm
    eye = jnp.where(row == colm, 1.0, 0.0)
    bodies = [(cc, jv) for cc in range(nchunk) for jv in range(gsz)]

    kk, qk = {}, {}
    for cc in range(nchunk):
        rows = slice(cc * c, (cc + 1) * c)
        for jq in range(gsz // rep):
            kb = k_ref[0, jq, rows, :].astype(BF16)
            qb = q_ref[0, jq, rows, :].astype(BF16)
            kk[cc, jq] = lax.dot_general(kb, kb, _NT, preferred_element_type=F32)
            qk[cc, jq] = lax.dot_general(qb, kb, _NT, preferred_element_type=F32)

    x, inv, rhs = {}, {}, {}
    for cc, jv in bodies:
        rows = slice(cc * c, (cc + 1) * c)
        jq = jv // rep
        gc = gc_ref[0, 0, rows, jv:jv + 1]
        gr = gr_ref[0, 0, jv:jv + 1, rows]
        beta = bc_ref[0, 0, rows, jv:jv + 1]
        decay = jnp.exp(jnp.where(tril, gc - gr, NEG_BIG))
        a = jnp.where(strict, kk[cc, jq] * decay, 0.0) * beta
        x[cc, jv] = a
        inv[cc, jv] = eye - a
        eg = jnp.exp(gc)
        k = k_ref[0, jq, rows, :]
        rhs[cc, jv] = jnp.concatenate([v_ref[0, jv, rows, :] * beta, k * (beta * eg)], axis=1).astype(BF16)
        g_last = gc[c - 1:c, :]
        kd_t = kt_ref[0, jq, :, rows] * jnp.exp(g_last - gr)
        l2_ref[0, jv, cc] = jnp.concatenate([qk[cc, jq] * decay, kd_t], axis=0).astype(l2_ref.dtype)
        l1_ref[0, jv, cc, c:2 * c, :] = (q_ref[0, jq, rows, :] * eg).astype(l1_ref.dtype)
        eg_ref[0, jv, cc] = jnp.broadcast_to(jnp.exp(g_last), (1, d))

    span = 2
    while span < c:
        for key in bodies:
            xb = x[key].astype(BF16)
            x[key] = jnp.dot(xb, xb, preferred_element_type=F32)
        for key in bodies:
            inv[key] = inv[key] + jnp.dot(inv[key].astype(BF16), x[key].astype(BF16), preferred_element_type=F32)
        span *= 2

    for cc, jv in bodies:
        rows = slice(cc * c, (cc + 1) * c)
        sol = jnp.dot(inv[cc, jv].astype(BF16), rhs[cc, jv], preferred_element_type=F32)
        u_ref[0, jv, rows, :] = sol[:, :d]
        l1_ref[0, jv, cc, 0:c, :] = sol[:, d:].astype(l1_ref.dtype)


def dn_local(q, k, kt, v, gc_col, gc_row, beta_col, *, gsz, tb):
    b, hv, t, d = v.shape
    c = DN_CHUNK
    rep = DN_V_HEADS // DN_QK_HEADS
    nb = t // tb
    nchunk = tb // c
    qk_spec = pl.BlockSpec((1, gsz // rep, tb, d), lambda bi, gi, ni: (bi, gi, ni, 0))
    col_spec = pl.BlockSpec((1, 1, tb, gsz), lambda bi, gi, ni: (bi, gi, ni, 0))

    def chunked(rows, lanes):
        return pl.BlockSpec((1, gsz, nchunk, rows, lanes), lambda bi, gi, ni: (bi, gi, ni, 0, 0))

    return pl.pallas_call(
        functools.partial(_dn_local_body, gsz=gsz, nchunk=nchunk),
        grid=(b, hv // gsz, nb),
        in_specs=[
            qk_spec, qk_spec,
            pl.BlockSpec((1, gsz // rep, d, tb), lambda bi, gi, ni: (bi, gi, 0, ni)),
            pl.BlockSpec((1, gsz, tb, d), lambda bi, gi, ni: (bi, gi, ni, 0)),
            col_spec,
            pl.BlockSpec((1, 1, gsz, tb), lambda bi, gi, ni: (bi, gi, 0, ni)),
            col_spec,
        ],
        out_specs=[
            pl.BlockSpec((1, gsz, tb, d), lambda bi, gi, ni: (bi, gi, ni, 0)),
            chunked(2 * c, d), chunked(c + d, c), chunked(1, d),
        ],
        out_shape=[
            jax.ShapeDtypeStruct((b, hv, t, d), F32),
            jax.ShapeDtypeStruct((b, hv, t // c, 2 * c, d), BF16),
            jax.ShapeDtypeStruct((b, hv, t // c, c + d, c), BF16),
            jax.ShapeDtypeStruct((b, hv, t // c, 1, d), F32),
        ],
        compiler_params=_cparams("parallel", "parallel", "parallel"),
        name="dn_local",
    )(q, k, kt, v, gc_col, gc_row, beta_col)


def _dn_scan_body(u_ref, l1_ref, l2_ref, eg_ref, z_ref, gain_ref, o_ref, state_ref, *, gsz, nchunk):
    c, d = DN_CHUNK, DN_HEAD_DIM

    @pl.when(pl.program_id(2) == 0)
    def _():
        state_ref[...] = jnp.zeros(state_ref.shape, F32)

    gain = gain_ref[...]
    heads = range(gsz)
    for cc in range(nchunk):
        rows = slice(cc * c, (cc + 1) * c)
        r1 = [jnp.dot(l1_ref[0, jv, cc], state_ref[jv].astype(BF16), preferred_element_type=F32) for jv in heads]
        v_new = [(u_ref[0, jv, rows, :] - r1[jv][:c]).astype(BF16) for jv in heads]
        r2 = [jnp.dot(l2_ref[0, jv, cc], v_new[jv], preferred_element_type=F32) for jv in heads]
        for jv in heads:
            state_ref[jv] = state_ref[jv] * eg_ref[0, jv, cc] + r2[jv][c:]
        for jv in heads:
            o = r1[jv][c:] + r2[jv][:c]
            ms = jnp.mean(o * o, axis=-1, keepdims=True)
            z = z_ref[rows, jv * d:(jv + 1) * d]
            o_ref[rows, jv * d:(jv + 1) * d] = (
                o * lax.rsqrt(ms + RMS_EPS) * gain * (z * jax.nn.sigmoid(z))).astype(o_ref.dtype)


def dn_scan(u, l1, l2, eg, proj, z_col0, norm_gain, *, gsz, tb):
    b, hv, t, d = u.shape
    c = DN_CHUNK
    nb = t // tb
    nchunk = tb // c
    zb0 = z_col0 // (gsz * d)

    def chunked(rows, lanes):
        return pl.BlockSpec((1, gsz, nchunk, rows, lanes), lambda bi, gi, ni: (bi, gi, ni, 0, 0))

    return pl.pallas_call(
        functools.partial(_dn_scan_body, gsz=gsz, nchunk=nchunk),
        grid=(b, hv // gsz, nb),
        in_specs=[
            pl.BlockSpec((1, gsz, tb, d), lambda bi, gi, ni: (bi, gi, ni, 0)),
            chunked(2 * c, d), chunked(c + d, c), chunked(1, d),
            pl.BlockSpec((tb, gsz * d), lambda bi, gi, ni: (bi * nb + ni, zb0 + gi)),
            pl.BlockSpec((1, d), lambda bi, gi, ni: (0, 0)),
        ],
        out_specs=pl.BlockSpec((tb, gsz * d), lambda bi, gi, ni: (bi * nb + ni, gi)),
        out_shape=jax.ShapeDtypeStruct((b * t, hv * d), BF16),
        scratch_shapes=[pltpu.VMEM((gsz, d, d), F32)],
        compiler_params=_cparams("parallel", "parallel", "arbitrary"),
        name="dn_scan",
    )(u, l1, l2, eg, proj, norm_gain.reshape(1, d))


def gdn_mixer(h2d, b, t, norm_gain, w_in, conv_w, a_log, dt_bias, out_gain, w_out):
    key_dim = DN_QK_HEADS * DN_HEAD_DIM
    val_dim = DN_V_HEADS * DN_HEAD_DIM
    main = 2 * key_dim + 2 * val_dim
    proj = norm_matmul(h2d, norm_gain, w_in[:, :main].astype(BF16), tm=1024, tn=1024)
    w_ba_t = _pad_cols(w_in[:, main:], LANES).T.astype(BF16)
    ba_t, _ = peer_query(h2d, norm_gain, w_ba_t, tm=512, tn=LANES)
    beta_t, gcum_t = dn_gates(ba_t, a_log, dt_bias, tm=512)
    q = dn_conv_silu(proj, conv_w, b, t, mode="q", col0=0, heads=DN_QK_HEADS, tt=512, hb=8)
    k, kt = dn_conv_silu(proj, conv_w, b, t, mode="k", col0=key_dim, heads=DN_QK_HEADS, tt=512, hb=8)
    v = dn_conv_silu(proj, conv_w, b, t, mode="v", col0=2 * key_dim, heads=DN_V_HEADS, tt=512, hb=8)
    gsz = 8
    hg = DN_V_HEADS // gsz
    rows = lambda x: x.reshape(hg, gsz, b, t).transpose(2, 0, 1, 3)
    cols = lambda x: rows(x).transpose(0, 1, 3, 2)
    u, l1, l2, eg = dn_local(q, k, kt, v, cols(gcum_t), rows(gcum_t), cols(beta_t), gsz=gsz, tb=4 * DN_CHUNK)
    o = dn_scan(u, l1, l2, eg, proj, 2 * key_dim + val_dim, out_gain, gsz=2 * gsz, tb=4 * DN_CHUNK)
    return matmul_residual(o, w_out.astype(BF16), h2d, tm=1024, tn=512)


def kernel(x, p, norm_mix, norm_ffn, norm_ple, attn_w_in, attn_q_norm, attn_k_norm, attn_w_out, dn_w_in, dn_conv,
           dn_a_log, dn_dt_bias, dn_norm, dn_w_out, peer_w_q, peer_keys, peer_u, peer_v, ple_w_in, ple_w_gate):
    b, t, d = x.shape
    n_mixers = 2
    h = x.reshape(b * t, d)
    for i in range(p.shape[0]):
        j = i // n_mixers
        if i % n_mixers == 0:
            h = dsa_mixer(h, b, t, norm_mix[i], attn_w_in[j], attn_q_norm[j], attn_k_norm[j], attn_w_out[j])
        else:
            h = gdn_mixer(h, b, t, norm_mix[i], dn_w_in[j], dn_conv[j], dn_a_log[j], dn_dt_bias[j], dn_norm[j],
                          dn_w_out[j])
        h = peer_mixer(h, norm_ffn[i], i, peer_w_q, peer_keys[i], peer_u, peer_v)
        h = ple_mixer(h, norm_ple[i], ple_w_gate[i], p[i].reshape(b * t, -1), ple_w_in[i], tm=1024, tn=512)
    return h.reshape(b, t, d)
```

```python
import functools

import jax
import jax.numpy as jnp
import numpy as np
from jax import lax
from jax.experimental import pallas as pl
from jax.experimental.pallas import tpu as pltpu

F32 = jnp.float32
BF16 = jnp.bfloat16
I32 = jnp.int32
HALF16 = 2 ** 15
TOP16_MASK = -(2 ** 16)
NEG_INF_BITS = -(2 ** 23)

RMS_EPS = 1e-6
ROPE_THETA = 500000.0
ROT_FRACTION = 4

N_HEADS = 16
N_KV_HEADS = 4
HEAD_DIM = 128
IDX_HEADS = 16
IDX_DIM = 64
INDEX_TOPK = 256
V_ROWS = HEAD_DIM + 16

DN_QK_HEADS = 16
DN_V_HEADS = 32
DN_HEAD_DIM = 128
DN_CONV_WIDTH = 4
DN_CHUNK = 64

PEER_HEADS = 8
PEER_NKEYS = 128
PEER_TOPK = 16

LANES = 128
VMEM_LIMIT = 56 * 1024 * 1024

LOG2E = 1.4426950408889634
INT_MIN = -(2 ** 31)
NEG_BIG = -1e30

_NT = (((1,), (1,)), ((), ()))


def _cparams(*sem):
    return pltpu.CompilerParams(dimension_semantics=sem, vmem_limit_bytes=VMEM_LIMIT)


def _norm_mm_body(x_ref, g_ref, w_ref, o_ref, xn_ref):
    @pl.when(pl.program_id(1) == 0)
    def _():
        x = x_ref[...]
        ms = jnp.mean(x * x, axis=-1, keepdims=True)
        xn_ref[...] = (x * lax.rsqrt(ms + RMS_EPS) * g_ref[...]).astype(BF16)

    o_ref[...] = jnp.dot(xn_ref[...], w_ref[...], preferred_element_type=F32).astype(o_ref.dtype)


def norm_matmul(x, gain, w, *, tm, tn, out_dtype=F32):
    m, k = x.shape
    n = w.shape[1]
    return pl.pallas_call(
        _norm_mm_body,
        grid=(m // tm, n // tn),
        in_specs=[
            pl.BlockSpec((tm, k), lambda i, j: (i, 0)),
            pl.BlockSpec((1, k), lambda i, j: (0, 0)),
            pl.BlockSpec((k, tn), lambda i, j: (0, j)),
        ],
        out_specs=pl.BlockSpec((tm, tn), lambda i, j: (i, j)),
        out_shape=jax.ShapeDtypeStruct((m, n), out_dtype),
        scratch_shapes=[pltpu.VMEM((tm, k), BF16)],
        compiler_params=_cparams("parallel", "arbitrary"),
        name="norm_matmul",
    )(x, gain.reshape(1, k), w)


def _mm_res_body(x_ref, w_ref, r_ref, o_ref):
    o_ref[...] = r_ref[...] + jnp.dot(x_ref[...], w_ref[...], preferred_element_type=F32)


def matmul_residual(x, w, res, *, tm, tn):
    m, k = x.shape
    n = w.shape[1]
    return pl.pallas_call(
        _mm_res_body,
        grid=(m // tm, n // tn),
        in_specs=[
            pl.BlockSpec((tm, k), lambda i, j: (i, 0)),
            pl.BlockSpec((k, tn), lambda i, j: (0, j)),
            pl.BlockSpec((tm, tn), lambda i, j: (i, j)),
        ],
        out_specs=pl.BlockSpec((tm, tn), lambda i, j: (i, j)),
        out_shape=jax.ShapeDtypeStruct((m, n), F32),
        compiler_params=_cparams("parallel", "arbitrary"),
        name="matmul_residual",
    )(x, w, res)


def _rope_tables(t, head_dim):
    rot = head_dim // ROT_FRACTION
    half = rot // 2
    inv_freq = ROPE_THETA ** (-jnp.arange(half, dtype=F32) * (2.0 / rot))
    ang = jnp.arange(t, dtype=jnp.int32).astype(F32)[:, None] * inv_freq[None, :]
    cos, sin = jnp.cos(ang), jnp.sin(ang)
    rest = head_dim - rot
    c = jnp.concatenate([cos, cos, jnp.ones((t, rest), F32)], axis=-1)
    s_lo = jnp.concatenate([-sin, jnp.zeros((t, half + rest), F32)], axis=-1)
    s_hi = jnp.concatenate([jnp.zeros((t, half), F32), sin, jnp.zeros((t, rest), F32)], axis=-1)
    reps = LANES // head_dim
    return tuple(jnp.tile(a, (1, reps)) for a in (c, s_lo, s_hi)), half


def _rope_tile(x, c, s_lo, s_hi, half):
    return x * c + pltpu.roll(x, LANES - half, 1) * s_lo + pltpu.roll(x, half, 1) * s_hi


def _attn_prep_body(p_ref, qg_ref, kg_ref, c128_ref, sl128_ref, sh128_ref, c64_ref, sl64_ref, sh64_ref,
                    q_ref, k_ref, vt_ref, iq_ref, ik_ref, *, half128, half64, qb):
    c128, sl128, sh128 = c128_ref[...], sl128_ref[...], sh128_ref[...]
    c64, sl64, sh64 = c64_ref[...], sl64_ref[...], sh64_ref[...]

    def normed(x, g):
        ms = jnp.mean(x * x, axis=-1, keepdims=True)
        return x * lax.rsqrt(ms + RMS_EPS) * g

    off = 0
    nblk = p_ref.shape[0] // qb
    group = N_HEADS // N_KV_HEADS
    for h in range(N_HEADS):
        x = normed(p_ref[:, off:off + HEAD_DIM], qg_ref[...])
        xt = (_rope_tile(x, c128, sl128, sh128, half128) * (HEAD_DIM ** -0.5 * LOG2E)).T.astype(BF16)
        n, g = divmod(h, group)
        for j in range(nblk):
            c0 = (j * group + g) * qb
            q_ref[0, n, :, c0:c0 + qb] = xt[:, j * qb:(j + 1) * qb]
        off += HEAD_DIM
    for h in range(N_KV_HEADS):
        x = normed(p_ref[:, off:off + HEAD_DIM], kg_ref[...])
        k_ref[0, h] = _rope_tile(x, c128, sl128, sh128, half128).astype(BF16)
        off += HEAD_DIM
    ones_rows = jnp.ones((V_ROWS - HEAD_DIM, p_ref.shape[0]), BF16)
    for h in range(N_KV_HEADS):
        vt_ref[0, h, 0:HEAD_DIM, :] = p_ref[:, off:off + HEAD_DIM].T.astype(BF16)
        vt_ref[0, h, HEAD_DIM:V_ROWS, :] = ones_rows
        off += HEAD_DIM
    for j in range(IDX_HEADS * IDX_DIM // LANES):
        xt = (_rope_tile(p_ref[:, off:off + LANES], c64, sl64, sh64, half64) * (IDX_DIM ** -0.5)).T.astype(BF16)
        for u in range(LANES // IDX_DIM):
            h = j * (LANES // IDX_DIM) + u
            for jb in range(nblk):
                c0 = (jb * IDX_HEADS + h) * qb
                iq_ref[0, :, c0:c0 + qb] = xt[u * IDX_DIM:(u + 1) * IDX_DIM, jb * qb:(jb + 1) * qb]
        off += LANES
    x = _rope_tile(p_ref[:, off:off + LANES], c64, sl64, sh64, half64)
    ik_ref[0] = x[:, :IDX_DIM].astype(BF16)


def attn_prep(proj, q_gain, k_gain, b, t, *, tt, qb):
    (c128, sl128, sh128), half128 = _rope_tables(t, HEAD_DIM)
    (c64, sl64, sh64), half64 = _rope_tables(t, IDX_DIM)
    nt = t // tt
    width = proj.shape[1]
    group = N_HEADS // N_KV_HEADS
    tab = pl.BlockSpec((tt, LANES), lambda bi, ti: (ti, 0))
    gain = pl.BlockSpec((1, HEAD_DIM), lambda bi, ti: (0, 0))
    return pl.pallas_call(
        functools.partial(_attn_prep_body, half128=half128, half64=half64, qb=qb),
        grid=(b, nt),
        in_specs=[pl.BlockSpec((tt, width), lambda bi, ti: (bi * nt + ti, 0)), gain, gain,
                  tab, tab, tab, tab, tab, tab],
        out_specs=[
            pl.BlockSpec((1, N_KV_HEADS, HEAD_DIM, group * tt), lambda bi, ti: (bi, 0, 0, ti)),
            pl.BlockSpec((1, N_KV_HEADS, tt, HEAD_DIM), lambda bi, ti: (bi, 0, ti, 0)),
            pl.BlockSpec((1, N_KV_HEADS, V_ROWS, tt), lambda bi, ti: (bi, 0, 0, ti)),
            pl.BlockSpec((1, IDX_DIM, IDX_HEADS * tt), lambda bi, ti: (bi, 0, ti)),
            pl.BlockSpec((1, tt, IDX_DIM), lambda bi, ti: (bi, ti, 0)),
        ],
        out_shape=[
            jax.ShapeDtypeStruct((b, N_KV_HEADS, HEAD_DIM, group * t), BF16),
            jax.ShapeDtypeStruct((b, N_KV_HEADS, t, HEAD_DIM), BF16),
            jax.ShapeDtypeStruct((b, N_KV_HEADS, V_ROWS, t), BF16),
            jax.ShapeDtypeStruct((b, IDX_DIM, IDX_HEADS * t), BF16),
            jax.ShapeDtypeStruct((b, t, IDX_DIM), BF16),
        ],
        compiler_params=_cparams("parallel", "parallel"),
        name="attn_prep",
    )(proj, q_gain.reshape(1, HEAD_DIM), k_gain.reshape(1, HEAD_DIM), c128, sl128, sh128, c64, sl64, sh64)


def _attn_body(iq_ref, iw_ref, ik_ref, q_ref, k_ref, vt_ref, o_ref,
               keys_ref, a_ref, b_ref, m_ref, acc_ref, *, qb, kc, topk):
    qi = pl.program_id(1)
    group = N_HEADS // N_KV_HEADS
    nck = ((qi + 1) * qb + kc - 1) // kc
    iq = iq_ref[0]
    iw = iw_ref[0, 0] * (IDX_HEADS ** -0.5)
    qpos = qi * qb + lax.broadcasted_iota(I32, (kc, qb), 1)
    krow = lax.broadcasted_iota(I32, (kc, qb), 0)

    def score_chunk(c, carry):
        off = pl.multiple_of(c * kc, kc)
        ikc = ik_ref[0, pl.ds(off, kc), :]
        s = jnp.dot(ikc, iq, preferred_element_type=F32)
        s = jnp.maximum(s, 0.0) * iw
        sc = s[:, 0:qb]
        for h in range(1, IDX_HEADS):
            sc = sc + s[:, h * qb:(h + 1) * qb]
        sc = jnp.where(sc == 0.0, 0.0, sc)
        bits = pltpu.bitcast(sc, I32)
        key = bits ^ ((bits >> 31) & 0x7FFFFFFF)
        causal = krow + off <= qpos
        keys_ref[c] = jnp.where(causal, key, INT_MIN)
        top = jnp.where(causal, bits & TOP16_MASK, NEG_INF_BITS)
        half = pl.ds(pl.multiple_of((c % 2) * kc, kc), kc)
        a_ref[c // 2, half, :] = pltpu.bitcast(top, F32).astype(BF16)
        return carry

    lax.fori_loop(0, nck, score_chunk, 0)

    @pl.when(nck % 2 == 1)
    def _():
        keys_ref[nck] = jnp.full((kc, qb), INT_MIN, I32)
        a_ref[nck // 2, kc:2 * kc, :] = jnp.full((kc, qb), -jnp.inf, BF16)

    npair = (nck + 1) // 2

    def count_ge(ref, cand):
        candb = jnp.broadcast_to(cand, (16, qb))
        one, nil = jnp.ones((), ref.dtype), jnp.zeros((), ref.dtype)

        def body(c, acc):
            parts = [jnp.where(ref[c, r * 16:(r + 1) * 16, :] >= candb, one, nil) for r in range(2 * kc // 16)]
            while len(parts) > 1:
                parts = [parts[i] + parts[i + 1] for i in range(0, len(parts), 2)]
            return acc + parts[0].astype(F32)

        acc = lax.fori_loop(0, npair, body, jnp.zeros((16, qb), F32))
        return acc.sum(axis=0, keepdims=True)

    def top_as_float(v):
        pattern = v ^ ((v >> 31) & 0x7FFF)
        return pltpu.bitcast(lax.shift_left(pattern, 16), F32).astype(BF16)

    def small_int(v):
        return v.astype(F32).astype(BF16)

    def search(ref, need, nbits, start, to_ref):
        def bit_step(i, ans):
            cand = ans | lax.shift_left(jnp.int32(1), nbits - 1 - i)
            return jnp.where(count_ge(ref, to_ref(cand)) >= need, cand, ans)

        return lax.fori_loop(0, nbits, bit_step, start)

    kf = jnp.float32(topk)
    zero = jnp.zeros((1, qb), I32)
    ans_hi = jnp.where(count_ge(a_ref, top_as_float(zero)) >= kf, zero, -HALF16)
    ans_hi = search(a_ref, kf, 15, ans_hi, top_as_float)
    above = jnp.where(ans_hi == HALF16 - 1, 0.0,
                      count_ge(a_ref, top_as_float(jnp.minimum(ans_hi + 1, HALF16 - 1))))

    def split_low(c, carry):
        for hf in range(2):
            k = keys_ref[2 * c + hf]
            rows = slice(hf * kc, (hf + 1) * kc)
            a_ref[c, rows, :] = small_int(jnp.where((k >> 16) == ans_hi, (k >> 8) & 0xFF, -1))
            b_ref[c, rows, :] = small_int(k & 0xFF)
        return carry

    lax.fori_loop(0, npair, split_low, 0)
    ans_mid = search(a_ref, kf - above, 8, zero, small_int)
    above = above + jnp.where(ans_mid == 255, 0.0, count_ge(a_ref, small_int(jnp.minimum(ans_mid + 1, 255))))
    mid_b = jnp.broadcast_to(small_int(ans_mid), (2 * kc, qb))

    def mask_low(c, carry):
        b_ref[c] = jnp.where(a_ref[c] == mid_b, b_ref[c], -1.0).astype(b_ref.dtype)
        return carry

    lax.fori_loop(0, npair, mask_low, 0)
    ans_lo = search(b_ref, kf - above, 8, zero, small_int)
    tau = lax.shift_left(ans_hi, 16) | lax.shift_left(ans_mid, 8) | ans_lo
    tau = jnp.maximum(tau, INT_MIN + 1)

    m_ref[...] = jnp.full(m_ref.shape, NEG_BIG, F32)
    acc_ref[...] = jnp.zeros(acc_ref.shape, F32)

    def attn_chunk(c, carry):
        off = pl.multiple_of(c * kc, kc)
        bias = jnp.where(keys_ref[c] >= tau, 0.0, NEG_BIG)
        bias = jnp.concatenate([bias] * group, axis=1)
        heads = range(N_KV_HEADS)
        s = [jnp.dot(k_ref[0, n, pl.ds(off, kc), :], q_ref[0, n], preferred_element_type=F32) + bias
             for n in heads]
        m_new = [jnp.maximum(m_ref[n], s[n].max(axis=0, keepdims=True)) for n in heads]
        p = [jnp.exp2(s[n] - m_new[n]) for n in heads]
        pv = [jnp.dot(vt_ref[0, n, :, pl.ds(off, kc)], p[n].astype(BF16), preferred_element_type=F32)
              for n in heads]
        for n in heads:
            acc_ref[n] = jnp.exp2(m_ref[n] - m_new[n]) * acc_ref[n] + pv[n]
            m_ref[n] = m_new[n]
        return carry

    lax.fori_loop(0, nck, attn_chunk, 0)

    for n in range(N_KV_HEADS):
        on = acc_ref[n, 0:HEAD_DIM, :] / acc_ref[n, HEAD_DIM:HEAD_DIM + 1, :]
        for g in range(group):
            h = n * group + g
            o_ref[0, :, h * HEAD_DIM:(h + 1) * HEAD_DIM] = on[:, g * qb:(g + 1) * qb].T.astype(o_ref.dtype)


def sparse_attention(q, k, vt, iq, ik, iw, *, qb, kc):
    b, _, t, _ = k.shape
    nq = t // qb
    topk = min(INDEX_TOPK, t // 4)
    group = N_HEADS // N_KV_HEADS
    iw_rows = iw.reshape(b, nq, qb, IDX_HEADS).transpose(0, 1, 3, 2).reshape(b, nq, 1, IDX_HEADS * qb)
    resident = pl.Buffered(1)
    return pl.pallas_call(
        functools.partial(_attn_body, qb=qb, kc=kc, topk=topk),
        grid=(b, nq),
        in_specs=[
            pl.BlockSpec((1, IDX_DIM, IDX_HEADS * qb), lambda bi, qi: (bi, 0, qi)),
            pl.BlockSpec((1, 1, 1, IDX_HEADS * qb), lambda bi, qi: (bi, qi, 0, 0)),
            pl.BlockSpec((1, t, IDX_DIM), lambda bi, qi: (bi, 0, 0), pipeline_mode=resident),
            pl.BlockSpec((1, N_KV_HEADS, HEAD_DIM, group * qb), lambda bi, qi: (bi, 0, 0, qi)),
            pl.BlockSpec((1, N_KV_HEADS, t, HEAD_DIM), lambda bi, qi: (bi, 0, 0, 0), pipeline_mode=resident),
            pl.BlockSpec((1, N_KV_HEADS, V_ROWS, t), lambda bi, qi: (bi, 0, 0, 0), pipeline_mode=resident),
        ],
        out_specs=pl.BlockSpec((1, qb, N_HEADS * HEAD_DIM), lambda bi, qi: (bi, qi, 0)),
        out_shape=jax.ShapeDtypeStruct((b, t, N_HEADS * HEAD_DIM), BF16),
        scratch_shapes=[
            pltpu.VMEM((t // kc, kc, qb), I32),
            pltpu.VMEM((t // (2 * kc), 2 * kc, qb), BF16),
            pltpu.VMEM((t // (2 * kc), 2 * kc, qb), BF16),
            pltpu.VMEM((N_KV_HEADS, 1, group * qb), F32),
            pltpu.VMEM((N_KV_HEADS, V_ROWS, group * qb), F32),
        ],
        compiler_params=_cparams("parallel", "arbitrary"),
        name="sparse_attention",
    )(iq, iw_rows, ik, q, k, vt)


def _pad_cols(w, mult):
    n = w.shape[1]
    pad = (-n) % mult
    return jnp.pad(w, ((0, 0), (0, pad))) if pad else w


def dsa_mixer(h2d, b, t, norm_gain, w_in, q_gain, k_gain, w_out):
    w = _pad_cols(w_in, LANES).astype(BF16)
    proj = norm_matmul(h2d, norm_gain, w, tm=1024, tn=w.shape[1] // 3)
    q, k, vt, iq, ik = attn_prep(proj, q_gain, k_gain, b, t, tt=256, qb=128)
    iw_off = N_HEADS * HEAD_DIM + 2 * N_KV_HEADS * HEAD_DIM + IDX_HEADS * IDX_DIM + IDX_DIM
    iw = proj[:, iw_off:iw_off + IDX_HEADS].reshape(b, t, IDX_HEADS)
    o = sparse_attention(q, k, vt, iq, ik, iw, qb=128, kc=512)
    return matmul_residual(o.reshape(b * t, -1), w_out.astype(BF16), h2d, tm=1024, tn=512)


def _peer_q_body(x_ref, g_ref, w_ref, qt_ref, xnt_ref):
    @pl.when(pl.program_id(1) == 0)
    def _():
        x = x_ref[...]
        ms = jnp.mean(x * x, axis=-1, keepdims=True)
        xnt_ref[...] = (x * lax.rsqrt(ms + RMS_EPS) * g_ref[...]).T.astype(BF16)

    qt_ref[...] = jnp.dot(w_ref[...], xnt_ref[...], preferred_element_type=F32)


def peer_query(h2d, gain, w_qt, *, tm, tn):
    m, k = h2d.shape
    n = w_qt.shape[0]
    return pl.pallas_call(
        _peer_q_body,
        grid=(m // tm, n // tn),
        in_specs=[
            pl.BlockSpec((tm, k), lambda i, j: (i, 0)),
            pl.BlockSpec((1, k), lambda i, j: (0, 0)),
            pl.BlockSpec((tn, k), lambda i, j: (j, 0)),
        ],
        out_specs=[
            pl.BlockSpec((tn, tm), lambda i, j: (j, i)),
            pl.BlockSpec((k, tm), lambda i, j: (0, i)),
        ],
        out_shape=[jax.ShapeDtypeStruct((n, m), F32), jax.ShapeDtypeStruct((k, m), BF16)],
        compiler_params=_cparams("parallel", "arbitrary"),
        name="peer_query",
    )(h2d, gain.reshape(1, k), w_qt)


def _top_values(x, dst_ref, n):
    rank = jnp.full(x.shape, float(n + 1), F32)
    for r in range(n):
        m = x.max(axis=0, keepdims=True)
        dst_ref[r:r + 1, :] = m
        hit = x == m
        rank = jnp.where(hit, float(r + 1), rank)
        x = jnp.where(hit, -jnp.inf, x)
    return rank


def _peer_route_body(qt_ref, keys_ref, n_ref, e0_ref, rank_ref, e1_ref, a_ref, b_ref, cand_ref, f_ref):
    k = PEER_TOPK
    for h in range(PEER_HEADS):
        s, ranks = [], []
        for p, dst in ((0, a_ref), (1, b_ref)):
            r0 = (h * 2 + p) * PEER_NKEYS
            sp = jnp.dot(keys_ref[h, p], qt_ref[r0:r0 + PEER_NKEYS, :],
                         precision=lax.Precision.HIGHEST, preferred_element_type=F32)
            s.append(sp)
            ranks.append(_top_values(sp, dst, k))
        spans, off = [], 0
        for i in range(k):
            spans.append((off, k // (i + 1)))
            off += k // (i + 1)
        cand_ref[off - off % 8:, :] = jnp.full((cand_ref.shape[0] - off + off % 8, cand_ref.shape[1]), -jnp.inf, F32)
        for i, (o, n) in enumerate(spans):
            cand_ref[o:o + n, :] = a_ref[i:i + 1, :] + b_ref[0:n, :]
        _top_values(cand_ref[...], f_ref, k)
        f = f_ref[...]
        z = jnp.exp(f - f[0:1, :]).sum(axis=0, keepdims=True)
        tau = f[k - 1:k, :]
        partners = jnp.zeros(s[0].shape, F32)
        for i, (o, n) in enumerate(spans):
            cnt = jnp.where(cand_ref[o:o + n, :] >= tau, 1.0, 0.0).sum(axis=0, keepdims=True)
            partners = jnp.where(ranks[0] == float(i + 1), cnt, partners)
        n_ref[h] = partners
        e0_ref[h] = jnp.exp(s[0] - a_ref[0:1, :])
        rank_ref[h] = ranks[1].astype(rank_ref.dtype)
        e1_ref[h] = (jnp.exp(s[1] - b_ref[0:1, :]) / z).astype(e1_ref.dtype)


def peer_route(qt, keys, *, tt):
    n, m = qt.shape
    spec = pl.BlockSpec((PEER_HEADS, PEER_NKEYS, tt), lambda i: (0, 0, i))
    shp = jax.ShapeDtypeStruct((PEER_HEADS, PEER_NKEYS, m), F32)
    shp16 = jax.ShapeDtypeStruct((PEER_HEADS, PEER_NKEYS, m), BF16)
    n_cand = sum(PEER_TOPK // (i + 1) for i in range(PEER_TOPK))
    return pl.pallas_call(
        _peer_route_body,
        grid=(m // tt,),
        in_specs=[
            pl.BlockSpec((n, tt), lambda i: (0, i)),
            pl.BlockSpec(keys.shape, lambda i: (0, 0, 0, 0)),
        ],
        out_specs=[spec, spec, spec, spec],
        out_shape=[shp, shp, shp16, shp16],
        scratch_shapes=[
            pltpu.VMEM((PEER_TOPK, tt), F32),
            pltpu.VMEM((PEER_TOPK, tt), F32),
            pltpu.VMEM((-(-n_cand // 8) * 8, tt), F32),
            pltpu.VMEM((PEER_TOPK, tt), F32),
        ],
        compiler_params=_cparams("parallel"),
        name="peer_route",
    )(qt, keys)


def _gelu(x):
    return 0.5 * x * (1.0 + lax.erf(x * (2.0 ** -0.5)))


def _peer_expert_body(xnt_ref, u_ref, vt_ref, n_ref, e0_ref, rank_ref, e1_ref, h_ref, o_ref, acc_ref, coef_ref, *, rows):
    e = pl.program_id(1)

    @pl.when(e == 0)
    def _():
        acc_ref[...] = jnp.zeros(acc_ref.shape, F32)

    act = jnp.dot(u_ref[...], xnt_ref[...], preferred_element_type=F32)
    for ii in range(rows):
        w = None
        for hd in range(PEER_HEADS):
            npart = n_ref[hd, ii:ii + 1, :].astype(BF16)
            e0 = e0_ref[hd, ii:ii + 1, :].astype(BF16)
            t = jnp.where(rank_ref[hd] <= npart, e1_ref[hd], 0.0) * e0
            w = t if w is None else w + t
        sl = slice(ii * PEER_NKEYS, (ii + 1) * PEER_NKEYS)
        coef_ref[sl, :] = w * _gelu(act[sl, :]).astype(BF16)
    acc_ref[...] += jnp.dot(vt_ref[...], coef_ref[...], preferred_element_type=F32)

    @pl.when(e == pl.num_programs(1) - 1)
    def _():
        o_ref[...] = h_ref[...] + acc_ref[...].T


def peer_experts(xnt, u, vt, npart, e0, rank, e1, h2d, *, tt, te):
    d, m = xnt.shape
    n_exp = u.shape[0]
    rows = te // PEER_NKEYS
    row_spec = pl.BlockSpec((PEER_HEADS, rows, tt), lambda i, e: (0, e, i))
    col_spec = pl.BlockSpec((PEER_HEADS, PEER_NKEYS, tt), lambda i, e: (0, 0, i))
    tok_spec = pl.BlockSpec((tt, d), lambda i, e: (i, 0))
    return pl.pallas_call(
        functools.partial(_peer_expert_body, rows=rows),
        grid=(m // tt, n_exp // te),
        in_specs=[
            pl.BlockSpec((d, tt), lambda i, e: (0, i)),
            pl.BlockSpec((te, d), lambda i, e: (e, 0)),
            pl.BlockSpec((d, te), lambda i, e: (0, e)),
            row_spec, row_spec, col_spec, col_spec,
            tok_spec,
        ],
        out_specs=tok_spec,
        out_shape=jax.ShapeDtypeStruct((m, d), F32),
        scratch_shapes=[pltpu.VMEM((d, tt), F32), pltpu.VMEM((te, tt), BF16)],
        compiler_params=_cparams("parallel", "arbitrary"),
        name="peer_experts",
    )(xnt, u, vt, npart, e0, rank, e1, h2d)


def _layer_cast_body(x_ref, o_ref, *, transpose):
    x = x_ref[0]
    o_ref[...] = (x.T if transpose else x).astype(o_ref.dtype)


def layer_cast(x, layer, *, tr, transpose):
    _, r, c = x.shape
    return pl.pallas_call(
        functools.partial(_layer_cast_body, transpose=transpose),
        grid=(r // tr,),
        in_specs=[pl.BlockSpec((1, tr, c), lambda i: (layer, i, 0))],
        out_specs=pl.BlockSpec((c, tr), lambda i: (0, i)) if transpose else pl.BlockSpec((tr, c), lambda i: (i, 0)),
        out_shape=jax.ShapeDtypeStruct((c, r) if transpose else (r, c), BF16),
        compiler_params=_cparams("parallel"),
        name="layer_cast_t" if transpose else "layer_cast",
    )(x)


def peer_mixer(h2d, gain, layer, w_q, keys, u_tab, v_tab):
    qt, xnt = peer_query(h2d, gain, layer_cast(w_q, layer, tr=512, transpose=True), tm=1024, tn=512)
    npart, e0, rank, e1 = peer_route(qt, keys, tt=256)
    u16 = layer_cast(u_tab, layer, tr=512, transpose=False)
    vt16 = layer_cast(v_tab, layer, tr=512, transpose=True)
    return peer_experts(xnt, u16, vt16, npart, e0, rank, e1, h2d, tt=512, te=1024)


def _ple_body(x_ref, g_ref, wg_ref, p_ref, wi_ref, r_ref, o_ref, xn_ref):
    @pl.when(pl.program_id(1) == 0)
    def _():
        x = x_ref[...]
        ms = jnp.mean(x * x, axis=-1, keepdims=True)
        xn_ref[...] = (x * lax.rsqrt(ms + RMS_EPS) * g_ref[...]).astype(BF16)

    gate = jax.nn.sigmoid(jnp.dot(xn_ref[...], wg_ref[...], preferred_element_type=F32))
    emb = jnp.dot(p_ref[...], wi_ref[...], preferred_element_type=F32)
    o_ref[...] = r_ref[...] + gate * emb


def ple_mixer(h2d, gain, w_gate, p2d, w_in, *, tm, tn):
    m, d = h2d.shape
    pd = p2d.shape[1]
    return pl.pallas_call(
        _ple_body,
        grid=(m // tm, d // tn),
        in_specs=[
            pl.BlockSpec((tm, d), lambda i, j: (i, 0)),
            pl.BlockSpec((1, d), lambda i, j: (0, 0)),
            pl.BlockSpec((d, tn), lambda i, j: (0, j)),
            pl.BlockSpec((tm, pd), lambda i, j: (i, 0)),
            pl.BlockSpec((pd, tn), lambda i, j: (0, j)),
            pl.BlockSpec((tm, tn), lambda i, j: (i, j)),
        ],
        out_specs=pl.BlockSpec((tm, tn), lambda i, j: (i, j)),
        out_shape=jax.ShapeDtypeStruct((m, d), F32),
        scratch_shapes=[pltpu.VMEM((tm, d), BF16)],
        compiler_params=_cparams("parallel", "arbitrary"),
        name="ple_mixer",
    )(h2d, gain.reshape(1, d), w_gate.astype(BF16), p2d.astype(BF16), w_in.astype(BF16), h2d)


def _dn_conv_body(x_ref, halo_ref, w_ref, *rest, mode, tt, hb):
    if mode == "k":
        o_ref, ot_ref, ext_ref = rest
    else:
        o_ref, ext_ref = rest
    width, d = DN_CONV_WIDTH, DN_HEAD_DIM
    first = pl.program_id(1) == 0
    ext_ref[0:8, :] = jnp.where(first, 0.0, halo_ref[...])
    ext_ref[8:, :] = x_ref[...]
    for h in range(hb):
        lanes = slice(h * d, (h + 1) * d)
        y = None
        for j in range(width):
            term = w_ref[j:j + 1, lanes] * ext_ref[pl.ds(8 - (width - 1) + j, tt), lanes]
            y = term if y is None else y + term
        y = y * jax.nn.sigmoid(y)
        if mode in ("q", "k"):
            y = y * lax.rsqrt(jnp.sum(y * y, axis=-1, keepdims=True) + RMS_EPS)
        if mode == "q":
            y = y * (d ** -0.5)
        o_ref[0, h] = y.astype(o_ref.dtype)
        if mode == "k":
            ot_ref[0, h] = y.T.astype(ot_ref.dtype)


def dn_conv_silu(proj, conv_w, b, t, *, mode, col0, heads, tt, hb):
    nt = t // tt
    d = DN_HEAD_DIM
    cb0 = col0 // (hb * d)
    out_spec = pl.BlockSpec((1, hb, tt, d), lambda bi, ti, hi: (bi, hi, ti, 0))
    out_shape = jax.ShapeDtypeStruct((b, heads, t, d), F32)
    out_specs, out_shapes = [out_spec], [out_shape]
    if mode == "k":
        out_specs.append(pl.BlockSpec((1, hb, d, tt), lambda bi, ti, hi: (bi, hi, 0, ti)))
        out_shapes.append(jax.ShapeDtypeStruct((b, heads, d, t), F32))
    res = pl.pallas_call(
        functools.partial(_dn_conv_body, mode=mode, tt=tt, hb=hb),
        grid=(b, nt, heads // hb),
        in_specs=[
            pl.BlockSpec((tt, hb * d), lambda bi, ti, hi: (bi * nt + ti, cb0 + hi)),
            pl.BlockSpec((8, hb * d), lambda bi, ti, hi: (jnp.maximum((bi * nt + ti) * (tt // 8) - 1, 0), cb0 + hi)),
            pl.BlockSpec((DN_CONV_WIDTH, hb * d), lambda bi, ti, hi: (0, cb0 + hi)),
        ],
        out_specs=out_specs,
        out_shape=out_shapes,
        scratch_shapes=[pltpu.VMEM((tt + 8, hb * d), F32)],
        compiler_params=_cparams("parallel", "parallel", "parallel"),
        name="dn_conv_" + mode,
    )(proj, proj, conv_w)
    return res if mode == "k" else res[0]


def _dn_gate_body(ba_ref, alog_ref, dtb_ref, beta_ref, gcum_ref, *, tm):
    nh = DN_V_HEADS
    beta_ref[...] = jax.nn.sigmoid(ba_ref[0:nh, :])
    a = ba_ref[nh:2 * nh, :]
    g = -jnp.exp(alog_ref[...]) * jax.nn.softplus(a + dtb_ref[...])
    lane = lax.broadcasted_iota(I32, (nh, LANES), 1) % DN_CHUNK
    for j in range(tm // LANES):
        x = g[:, j * LANES:(j + 1) * LANES]
        s = 1
        while s < DN_CHUNK:
            x = x + jnp.where(lane >= s, pltpu.roll(x, s, 1), 0.0)
            s *= 2
        gcum_ref[:, j * LANES:(j + 1) * LANES] = x


def dn_gates(ba_t, a_log, dt_bias, *, tm):
    m = ba_t.shape[1]
    nh = DN_V_HEADS
    spec = pl.BlockSpec((nh, tm), lambda i: (0, i))
    col = pl.BlockSpec((nh, 1), lambda i: (0, 0))
    return pl.pallas_call(
        functools.partial(_dn_gate_body, tm=tm),
        grid=(m // tm,),
        in_specs=[pl.BlockSpec((ba_t.shape[0], tm), lambda i: (0, i)), col, col],
        out_specs=[spec, spec],
        out_shape=[jax.ShapeDtypeStruct((nh, m), F32)] * 2,
        compiler_params=_cparams("parallel"),
        name="dn_gates",
    )(ba_t, a_log.reshape(nh, 1), dt_bias.reshape(nh, 1))


def _dn_local_body(q_ref, k_ref, kt_ref, v_ref, gc_ref, gr_ref, bc_ref, u_ref, l1_ref, l2_ref, eg_ref, *, gsz, nchunk):
    c, d = DN_CHUNK, DN_HEAD_DIM
    rep = DN_V_HEADS // DN_QK_HEADS
    row = lax.broadcasted_iota(I32, (c, c), 0)
    colm = lax.broadcasted_iota(I32, (c, c), 1)
    tril, strict = row >= colm, row > colm
    eye = jnp.where(row == colm, 1.0, 0.0)
    bodies = [(cc, jv) for cc in range(nchunk) for jv in range(gsz)]

    kk, qk = {}, {}
    for cc in range(nchunk):
        rows = slice(cc * c, (cc + 1) * c)
        for jq in range(gsz // rep):
            kb = k_ref[0, jq, rows, :].astype(BF16)
            qb = q_ref[0, jq, rows, :].astype(BF16)
            kk[cc, jq] = lax.dot_general(kb, kb, _NT, preferred_element_type=F32)
            qk[cc, jq] = lax.dot_general(qb, kb, _NT, preferred_element_type=F32)

    x, inv, rhs = {}, {}, {}
    for cc, jv in bodies:
        rows = slice(cc * c, (cc + 1) * c)
        jq = jv // rep
        gc = gc_ref[0, 0, rows, jv:jv + 1]
        gr = gr_ref[0, 0, jv:jv + 1, rows]
        beta = bc_ref[0, 0, rows, jv:jv + 1]
        decay = jnp.exp(jnp.where(tril, gc - gr, NEG_BIG))
        a = jnp.where(strict, kk[cc, jq] * decay, 0.0) * beta
        x[cc, jv] = a
        inv[cc, jv] = eye - a
        eg = jnp.exp(gc)
        k = k_ref[0, jq, rows, :]
        rhs[cc, jv] = jnp.concatenate([v_ref[0, jv, rows, :] * beta, k * (beta * eg)], axis=1).astype(BF16)
        g_last = gc[c - 1:c, :]
        kd_t = kt_ref[0, jq, :, rows] * jnp.exp(g_last - gr)
        l2_ref[0, jv, cc] = jnp.concatenate([qk[cc, jq] * decay, kd_t], axis=0).astype(l2_ref.dtype)
        l1_ref[0, jv, cc, c:2 * c, :] = (q_ref[0, jq, rows, :] * eg).astype(l1_ref.dtype)
        eg_ref[0, jv, cc] = jnp.broadcast_to(jnp.exp(g_last), (1, d))

    span = 2
    while span < c:
        for key in bodies:
            xb = x[key].astype(BF16)
            x[key] = jnp.dot(xb, xb, preferred_element_type=F32)
        for key in bodies:
            inv[key] = inv[key] + jnp.dot(inv[key].astype(BF16), x[key].astype(BF16), preferred_element_type=F32)
        span *= 2

    for cc, jv in bodies:
        rows = slice(cc * c, (cc + 1) * c)
        sol = jnp.dot(inv[cc, jv].astype(BF16), rhs[cc, jv], preferred_element_type=F32)
        u_ref[0, jv, rows, :] = sol[:, :d]
        l1_ref[0, jv, cc, 0:c, :] = sol[:, d:].astype(l1_ref.dtype)


def dn_local(q, k, kt, v, gc_col, gc_row, beta_col, *, gsz, tb):
    b, hv, t, d = v.shape
    c = DN_CHUNK
    rep = DN_V_HEADS // DN_QK_HEADS
    nb = t // tb
    nchunk = tb // c
    qk_spec = pl.BlockSpec((1, gsz // rep, tb, d), lambda bi, gi, ni: (bi, gi, ni, 0))
    col_spec = pl.BlockSpec((1, 1, tb, gsz), lambda bi, gi, ni: (bi, gi, ni, 0))

    def chunked(rows, lanes):
        return pl.BlockSpec((1, gsz, nchunk, rows, lanes), lambda bi, gi, ni: (bi, gi, ni, 0, 0))

    return pl.pallas_call(
        functools.partial(_dn_local_body, gsz=gsz, nchunk=nchunk),
        grid=(b, hv // gsz, nb),
        in_specs=[
            qk_spec, qk_spec,
            pl.BlockSpec((1, gsz // rep, d, tb), lambda bi, gi, ni: (bi, gi, 0, ni)),
            pl.BlockSpec((1, gsz, tb, d), lambda bi, gi, ni: (bi, gi, ni, 0)),
            col_spec,
            pl.BlockSpec((1, 1, gsz, tb), lambda bi, gi, ni: (bi, gi, 0, ni)),
            col_spec,
        ],
        out_specs=[
            pl.BlockSpec((1, gsz, tb, d), lambda bi, gi, ni: (bi, gi, ni, 0)),
            chunked(2 * c, d), chunked(c + d, c), chunked(1, d),
        ],
        out_shape=[
            jax.ShapeDtypeStruct((b, hv, t, d), F32),
            jax.ShapeDtypeStruct((b, hv, t // c, 2 * c, d), BF16),
            jax.ShapeDtypeStruct((b, hv, t // c, c + d, c), BF16),
            jax.ShapeDtypeStruct((b, hv, t // c, 1, d), F32),
        ],
        compiler_params=_cparams("parallel", "parallel", "parallel"),
        name="dn_local",
    )(q, k, kt, v, gc_col, gc_row, beta_col)


def _dn_scan_body(u_ref, l1_ref, l2_ref, eg_ref, z_ref, gain_ref, o_ref, state_ref, *, gsz, nchunk):
    c, d = DN_CHUNK, DN_HEAD_DIM

    @pl.when(pl.program_id(2) == 0)
    def _():
        state_ref[...] = jnp.zeros(state_ref.shape, F32)

    gain = gain_ref[...]
    heads = range(gsz)
    for cc in range(nchunk):
        rows = slice(cc * c, (cc + 1) * c)
        r1 = [jnp.dot(l1_ref[0, jv, cc], state_ref[jv].astype(BF16), preferred_element_type=F32) for jv in heads]
        v_new = [(u_ref[0, jv, rows, :] - r1[jv][:c]).astype(BF16) for jv in heads]
        r2 = [jnp.dot(l2_ref[0, jv, cc], v_new[jv], preferred_element_type=F32) for jv in heads]
        for jv in heads:
            state_ref[jv] = state_ref[jv] * eg_ref[0, jv, cc] + r2[jv][c:]
        for jv in heads:
            o = r1[jv][c:] + r2[jv][:c]
            ms = jnp.mean(o * o, axis=-1, keepdims=True)
            z = z_ref[rows, jv * d:(jv + 1) * d]
            o_ref[rows, jv * d:(jv + 1) * d] = (
                o * lax.rsqrt(ms + RMS_EPS) * gain * (z * jax.nn.sigmoid(z))).astype(o_ref.dtype)


def dn_scan(u, l1, l2, eg, proj, z_col0, norm_gain, *, gsz, tb):
    b, hv, t, d = u.shape
    c = DN_CHUNK
    nb = t // tb
    nchunk = tb // c
    zb0 = z_col0 // (gsz * d)

    def chunked(rows, lanes):
        return pl.BlockSpec((1, gsz, nchunk, rows, lanes), lambda bi, gi, ni: (bi, gi, ni, 0, 0))

    return pl.pallas_call(
        functools.partial(_dn_scan_body, gsz=gsz, nchunk=nchunk),
        grid=(b, hv // gsz, nb),
        in_specs=[
            pl.BlockSpec((1, gsz, tb, d), lambda bi, gi, ni: (bi, gi, ni, 0)),
            chunked(2 * c, d), chunked(c + d, c), chunked(1, d),
            pl.BlockSpec((tb, gsz * d), lambda bi, gi, ni: (bi * nb + ni, zb0 + gi)),
            pl.BlockSpec((1, d), lambda bi, gi, ni: (0, 0)),
        ],
        out_specs=pl.BlockSpec((tb, gsz * d), lambda bi, gi, ni: (bi * nb + ni, gi)),
        out_shape=jax.ShapeDtypeStruct((b * t, hv * d), BF16),
        scratch_shapes=[pltpu.VMEM((gsz, d, d), F32)],
        compiler_params=_cparams("parallel", "parallel", "arbitrary"),
        name="dn_scan",
    )(u, l1, l2, eg, proj, norm_gain.reshape(1, d))


def gdn_mixer(h2d, b, t, norm_gain, w_in, conv_w, a_log, dt_bias, out_gain, w_out):
    key_dim = DN_QK_HEADS * DN_HEAD_DIM
    val_dim = DN_V_HEADS * DN_HEAD_DIM
    main = 2 * key_dim + 2 * val_dim
    proj = norm_matmul(h2d, norm_gain, w_in[:, :main].astype(BF16), tm=1024, tn=1024)
    w_ba_t = _pad_cols(w_in[:, main:], LANES).T.astype(BF16)
    ba_t, _ = peer_query(h2d, norm_gain, w_ba_t, tm=512, tn=LANES)
    beta_t, gcum_t = dn_gates(ba_t, a_log, dt_bias, tm=512)
    q = dn_conv_silu(proj, conv_w, b, t, mode="q", col0=0, heads=DN_QK_HEADS, tt=512, hb=8)
    k, kt = dn_conv_silu(proj, conv_w, b, t, mode="k", col0=key_dim, heads=DN_QK_HEADS, tt=512, hb=8)
    v = dn_conv_silu(proj, conv_w, b, t, mode="v", col0=2 * key_dim, heads=DN_V_HEADS, tt=512, hb=8)
    gsz = 8
    hg = DN_V_HEADS // gsz
    rows = lambda x: x.reshape(hg, gsz, b, t).transpose(2, 0, 1, 3)
    cols = lambda x: rows(x).transpose(0, 1, 3, 2)
    u, l1, l2, eg = dn_local(q, k, kt, v, cols(gcum_t), rows(gcum_t), cols(beta_t), gsz=gsz, tb=4 * DN_CHUNK)
    o = dn_scan(u, l1, l2, eg, proj, 2 * key_dim + val_dim, out_gain, gsz=2 * gsz, tb=4 * DN_CHUNK)
    return matmul_residual(o, w_out.astype(BF16), h2d, tm=1024, tn=512)


def kernel(x, p, norm_mix, norm_ffn, norm_ple, attn_w_in, attn_q_norm, attn_k_norm, attn_w_out, dn_w_in, dn_conv,
           dn_a_log, dn_dt_bias, dn_norm, dn_w_out, peer_w_q, peer_keys, peer_u, peer_v, ple_w_in, ple_w_gate):
    b, t, d = x.shape
    n_mixers = 2
    h = x.reshape(b * t, d)
    for i in range(p.shape[0]):
        j = i // n_mixers
        if i % n_mixers == 0:
            h = dsa_mixer(h, b, t, norm_mix[i], attn_w_in[j], attn_q_norm[j], attn_k_norm[j], attn_w_out[j])
        else:
            h = gdn_mixer(h, b, t, norm_mix[i], dn_w_in[j], dn_conv[j], dn_a_log[j], dn_dt_bias[j], dn_norm[j],
                          dn_w_out[j])
        h = peer_mixer(h, norm_ffn[i], i, peer_w_q, peer_keys[i], peer_u, peer_v)
        h = ple_mixer(h, norm_ple[i], ple_w_gate[i], p[i].reshape(b * t, -1), ple_w_in[i], tm=1024, tn=512)
    return h.reshape(b, t, d)
```
